```python
import math
import jax
import jax.numpy as jnp
from jax import lax
import numpy as np

D_MODEL = 1024
BATCH = 8
SEQ = 4096
DEPTH = 2
DEC_BATCH = 32
DEC_SEQ = 8
PAST_LEN = 16384
PAGE_SIZE = 128

N_META = 16
H_A = 4
DK_A = 64
DV_A = 64
W_A = H_A * DV_A
H_B = 4
DH_B = 64
DV_B = 2 * DH_B
W_B = H_B * DV_B
H_C = 4
DK_C = 64
DV_C = 64
W_C = H_C * DV_C
MIX_WIDTH = W_A + W_B + W_C
CONV_W = 4
HGRN_CHUNK = 64
MLSTM_CHUNK = 128
Q_BLOCK = 128
N_GROUPS = 4
EXPERTS_PER_GROUP = 8
N_EXPERTS = N_GROUPS * EXPERTS_PER_GROUP
D_EXPERT = 256
TOP_K_INNER = 2
MOE_BLOCK = 128
ALPHA = (2.0 * DEPTH) ** 0.25
BETA = (8.0 * DEPTH) ** -0.25
LN_EPS = 1e-5
NORM_EPS = 1e-6
PROJ_SIZES = (H_A * DK_A, H_A * DK_A, W_A, W_A,
              H_B * 2 * DH_B, H_B * 2 * DH_B, W_B,
              H_C * DK_C, H_C * DK_C, W_C, W_C, H_C, H_C)
VALUE_COLS = (2, 6, 9)
N_PROJ = sum(PROJ_SIZES)
PROJ_SPLITS = tuple(int(s) for s in np.cumsum(PROJ_SIZES)[:-1])

kernel_name = 'hymba_hgrn2_diffattn_mlstm_hmoe_step'


def _layernorm(x, g, b):
    xf = x.astype(jnp.float32)
    mu = jnp.mean(xf, -1, keepdims=True)
    var = jnp.mean(jnp.square(xf - mu), -1, keepdims=True)
    return ((xf - mu) * lax.rsqrt(var + LN_EPS) * g + b).astype(x.dtype)


def _head_rmsnorm(x, g, n_heads):
    B, L = x.shape[:2]
    xh = x.astype(jnp.float32).reshape(B, L, n_heads, -1)
    xh = xh * lax.rsqrt(jnp.mean(jnp.square(xh), -1, keepdims=True) + NORM_EPS)
    return xh.reshape(B, L, -1) * g


def _chunk_len(L, chunk):
    return chunk if L % chunk == 0 else L


def _to_chunks(a, chunk):
    B, L, H = a.shape[:3]
    n = L // chunk
    if a.ndim == 3:
        return a.reshape(B, n, chunk, H).transpose(1, 0, 3, 2)
    return a.reshape(B, n, chunk, H, a.shape[-1]).transpose(1, 0, 3, 2, 4)


def _from_chunks(o):
    n, B, H, C, d = o.shape
    return o.transpose(1, 0, 3, 2, 4).reshape(B, n * C, H, d)


def _hgrn2_chunks(q, k, v, log_f, S, chunk):
    causal = jnp.tril(jnp.ones((chunk, chunk), bool))

    def step(S, inp):
        q_, k_, v_, g_ = inp
        b = jnp.cumsum(g_, axis=2)
        diff = b[:, :, :, None, :] - b[:, :, None, :, :]
        decay = jnp.exp(jnp.where(causal[:, :, None], diff, -jnp.inf))
        att = jnp.einsum('bhtk,bhsk,bhtsk->bhts', q_, k_, decay)
        o = (jnp.einsum('bhts,bhsv->bhtv', att, v_)
             + jnp.einsum('bhtk,bhkv->bhtv', q_ * jnp.exp(b), S))
        b_end = b[:, :, -1:, :]
        S = (jnp.exp(b_end[:, :, 0, :])[..., None] * S
             + jnp.einsum('bhsk,bhsv->bhkv', k_ * jnp.exp(b_end - b), v_))
        return S, o

    xs = tuple(_to_chunks(a, chunk) for a in (q, k, v, log_f))
    S, o = lax.scan(step, S, xs)
    return _from_chunks(o), (S,)


def _mlstm_chunks(q, k, v, log_i, log_f, C, n, m, chunk):
    causal = jnp.tril(jnp.ones((chunk, chunk), bool))

    def step(carry, inp):
        C, n, m = carry
        q_, k_, v_, li, lf = inp
        F = jnp.cumsum(lf, axis=-1)
        log_d = jnp.where(causal, F[..., :, None] - F[..., None, :] + li[..., None, :], -jnp.inf)
        log_s = F + m[..., None]
        m_t = jnp.maximum(jnp.max(log_d, axis=-1), log_s)
        w = jnp.einsum('bhtd,bhsd->bhts', q_, k_) * jnp.exp(log_d - m_t[..., None])
        c_state = jnp.exp(log_s - m_t)
        num = (jnp.einsum('bhts,bhsv->bhtv', w, v_)
               + c_state[..., None] * jnp.einsum('bhvd,bhtd->bhtv', C, q_))
        den = jnp.sum(w, -1) + c_state * jnp.einsum('bhd,bhtd->bht', n, q_)
        h = num / jnp.maximum(jnp.abs(den), jnp.exp(-m_t))[..., None]
        m_new = m_t[..., -1]
        w_s = jnp.exp(F[..., -1:] - F + li - m_new[..., None])
        a = jnp.exp(F[..., -1] + m - m_new)
        C = a[..., None, None] * C + jnp.einsum('bhsv,bhsd->bhvd', v_ * w_s[..., None], k_)
        n = a[..., None] * n + jnp.einsum('bhs,bhsd->bhd', w_s, k_)
        return (C, n, m_new), h

    xs = tuple(_to_chunks(a, chunk) for a in (q, k, v, log_i, log_f))
    (C, n, m), h = lax.scan(step, (C, n, m), xs)
    return _from_chunks(h), (C, n, m)


def _segmented(fn, seqs, state, seg_lens, chunk):
    outs, start = [], 0
    for L in seg_lens:
        o, state = fn(*(a[:, start:start + L] for a in seqs), *state, _chunk_len(L, chunk))
        outs.append(o)
        start += L
    return jnp.concatenate(outs, axis=1), state


def _diff_core(q, k, v, mask, lam):
    s = jnp.einsum('bqhcd,bkhcd->bhcqk', q, k) * (DH_B ** -0.5)
    p = jax.nn.softmax(jnp.where(mask, s, -jnp.inf), axis=-1)
    w = p[:, :, 0] - lam * p[:, :, 1]
    return jnp.einsum('bhqk,bkhv->bqhv', w, v)


def _diff_attn_prompt(q, k, v, lam):
    B, L = q.shape[:2]
    n_blk = -(-L // Q_BLOCK)
    qp = jnp.pad(q, ((0, 0), (0, n_blk * Q_BLOCK - L), (0, 0), (0, 0), (0, 0)))
    qp = qp.reshape(B, n_blk, Q_BLOCK, H_B, 2, DH_B).transpose(1, 0, 2, 3, 4, 5)
    k_pos = jnp.arange(L)

    def block(args):
        qb, j = args
        q_pos = j * Q_BLOCK + jnp.arange(Q_BLOCK)
        return _diff_core(qb, k, v, q_pos[:, None] >= k_pos[None, :], lam)

    o = lax.map(block, (qp, jnp.arange(n_blk)))
    return o.transpose(1, 0, 2, 3, 4).reshape(B, n_blk * Q_BLOCK, H_B, DV_B)[:, :L]


def _diff_attn_sample(q, k_new, v_new, k_past, v_past, lam):
    B, L = q.shape[:2]
    P = k_past.shape[1]
    k = jnp.concatenate([k_past.astype(jnp.float32).reshape(B, P, H_B, 2, DH_B), k_new], axis=1)
    v = jnp.concatenate([v_past.astype(jnp.float32), v_new], axis=1)
    mask = (jnp.arange(L)[:, None] + P) >= jnp.arange(P + L)[None, :]
    return _diff_core(q, k, v, mask, lam)


def _token_mixer(h, p, lb, lam, lam_init, past_kv, seg_lens, S_a, C_c, n_c, m_c, conv_c):
    f32 = jnp.float32
    B, L, _ = h.shape
    proj = jnp.einsum('bld,dn->bln', h, p['w_in']) + p['b_in']
    qa, fa, ia, ga, qb, kb, vb, qc, kc, vc, oc, igc, fgc = jnp.split(proj, PROJ_SPLITS, axis=-1)

    lb = lb.reshape(H_A, DK_A)
    f = lb + (1.0 - lb) * jax.nn.sigmoid(fa.astype(f32).reshape(B, L, H_A, DK_A))
    o_a, (S_a,) = _segmented(
        _hgrn2_chunks,
        (qa.astype(f32).reshape(B, L, H_A, DK_A), 1.0 - f,
         ia.astype(f32).reshape(B, L, H_A, DV_A), jnp.log(f)),
        (S_a.astype(f32),), seg_lens, HGRN_CHUNK)
    o_a = _head_rmsnorm(o_a.reshape(B, L, W_A), p['norm_a'], H_A) * jax.nn.silu(ga.astype(f32))

    q_b = qb.astype(f32).reshape(B, L, H_B, 2, DH_B)
    k_b = kb.astype(f32).reshape(B, L, H_B, 2, DH_B)
    v_b = vb.astype(f32).reshape(B, L, H_B, DV_B)
    if past_kv is None:
        o_b = _diff_attn_prompt(q_b, k_b, v_b, lam)
    else:
        o_b = _diff_attn_sample(q_b, k_b, v_b, past_kv[0], past_kv[1], lam)
    o_b = (1.0 - lam_init) * _head_rmsnorm(o_b.reshape(B, L, W_B), p['norm_b'], H_B)
    k_rows = kb.reshape(B, L, H_B, 2 * DH_B)
    v_rows = vb.reshape(B, L, H_B, DV_B)

    qk_pre = jnp.concatenate([qc, kc], axis=-1)
    xp = jnp.concatenate([conv_c.astype(qk_pre.dtype), qk_pre], axis=1)
    qk = jax.nn.silu(p['conv_b'] + sum(xp[:, j:j + L] * p['conv_w'][j] for j in range(CONV_W)))
    conv_c = xp[:, L:]
    q_c, k_c = jnp.split(qk.astype(f32), 2, axis=-1)
    h_c, (C_c, n_c, m_c) = _segmented(
        _mlstm_chunks,
        (q_c.reshape(B, L, H_C, DK_C), k_c.reshape(B, L, H_C, DK_C) * (DK_C ** -0.5),
         vc.astype(f32).reshape(B, L, H_C, DV_C), igc.astype(f32),
         jax.nn.log_sigmoid(fgc.astype(f32) + p['b_mlstm_f'])),
        (C_c.astype(f32), n_c.astype(f32), m_c.astype(f32)), seg_lens, MLSTM_CHUNK)
    o_c = jax.nn.sigmoid(oc.astype(f32)) * _head_rmsnorm(h_c.reshape(B, L, W_C), p['norm_c'], H_C)

    mix = jnp.concatenate([o_a, o_b, o_c], axis=-1).astype(h.dtype)
    out = jnp.einsum('bln,nd->bld', mix, p['w_out'])
    return out, (k_rows, v_rows, S_a, C_c, n_c, m_c, conv_c)


def _expert_dispatch(x, expert, gate, w_g, w_u, w_d):
    T, D = x.shape
    E = w_g.shape[0]
    K = expert.shape[1]
    A = T * K
    flat_e = expert.reshape(-1)
    flat_tok = jnp.repeat(jnp.arange(T, dtype=jnp.int32), K)
    flat_w = gate.reshape(-1)
    order = jnp.argsort(flat_e)
    se, stok, sw = flat_e[order], flat_tok[order], flat_w[order]
    counts = jnp.zeros((E,), jnp.int32).at[flat_e].add(1)
    padded = (counts + MOE_BLOCK - 1) // MOE_BLOCK * MOE_BLOCK
    pad_end = jnp.cumsum(padded)
    pad_start = pad_end - padded
    start = jnp.cumsum(counts) - counts
    dest = pad_start[se] + jnp.arange(A, dtype=jnp.int32) - start[se]
    n_blocks = -(-A // MOE_BLOCK) + E
    R = n_blocks * MOE_BLOCK
    row_tok = jnp.full((R,), T, jnp.int32).at[dest].set(stok)
    row_w = jnp.zeros((R,), jnp.float32).at[dest].set(sw)
    blk_e = jnp.minimum(jnp.searchsorted(pad_end, jnp.arange(n_blocks) * MOE_BLOCK, side='right'), E - 1)
    x_pad = jnp.concatenate([x, jnp.zeros((1, D), x.dtype)], axis=0)
    xb = x_pad[row_tok].reshape(n_blocks, MOE_BLOCK, D)

    def block(args):
        xb_, e = args
        return (jax.nn.silu(xb_ @ w_g[e]) * (xb_ @ w_u[e])) @ w_d[e]

    yb = lax.map(block, (xb, blk_e)).reshape(R, D)
    return jnp.zeros((T + 1, D), jnp.float32).at[row_tok].add(yb * row_w[:, None])[:T]


def _hier_moe(h, p):
    B, L, D = h.shape
    x = h.reshape(-1, D)
    T = x.shape[0]
    xf = x.astype(jnp.float32)
    g_prob = jax.nn.softmax(xf @ p['w_rg'] + p['b_rg'], axis=-1)
    g_top, g_idx = lax.top_k(g_prob, 1)
    e_logits = (xf @ p['w_re'] + p['b_re']).reshape(T, N_GROUPS, EXPERTS_PER_GROUP)
    sel = jnp.broadcast_to(g_idx[:, :, None], (T, 1, EXPERTS_PER_GROUP))
    e_prob = jax.nn.softmax(jnp.take_along_axis(e_logits, sel, axis=1)[:, 0], axis=-1)
    top_p, top_i = lax.top_k(e_prob, TOP_K_INNER)
    gate = g_top * (top_p / jnp.sum(top_p, -1, keepdims=True))
    expert = g_idx * EXPERTS_PER_GROUP + top_i
    y = _expert_dispatch(x, expert, gate, p['w_eg'], p['w_eu'], p['w_ed'])
    return y.reshape(B, L, D).astype(h.dtype)


def _layer(h, p, lb, lam, lam_init, past_kv, seg_lens, S_a, C_c, n_c, m_c, conv_c):
    mix, new_state = _token_mixer(h, p, lb, lam, lam_init, past_kv, seg_lens, S_a, C_c, n_c, m_c, conv_c)
    h = _layernorm(ALPHA * h + mix, p['ln1_g'], p['ln1_b'])
    h = _layernorm(ALPHA * h + _hier_moe(h, p), p['ln2_g'], p['ln2_b'])
    return h, new_state


def setup_inputs(seed: int = 0) -> dict:
    key = jax.random.key(seed)
    ks = iter(jax.random.split(key, 48))

    def nrm(shape, scale):
        return jax.random.normal(next(ks), shape, jnp.float32) * scale

    n_pages = PAST_LEN // PAGE_SIZE
    n_phys = (DEC_BATCH * n_pages * 5) // 4
    page_table = jax.random.permutation(next(ks), n_phys)[:DEC_BATCH * n_pages]
    page_table = page_table.reshape(DEC_BATCH, n_pages).astype(jnp.int32)
    col_scale = jnp.concatenate([jnp.full((s,), BETA if i in VALUE_COLS else 1.0, jnp.float32)
                                 for i, s in enumerate(PROJ_SIZES)])
    return {
        'x_prompt': nrm((BATCH, SEQ, D_MODEL), 1.0),
        'x_sample': nrm((DEC_BATCH, DEC_SEQ, D_MODEL), 1.0),
        'cache_k': nrm((DEPTH, n_phys, PAGE_SIZE, H_B, 2 * DH_B), 1.0),
        'cache_v': nrm((DEPTH, n_phys, PAGE_SIZE, H_B, DV_B), 1.0),
        'page_table': page_table,
        'state_hgrn': nrm((DEPTH, DEC_BATCH, H_A, DK_A, DV_A), 1.0),
        'state_mlstm_c': nrm((DEPTH, DEC_BATCH, H_C, DV_C, DK_C), 1.0),
        'state_mlstm_n': nrm((DEPTH, DEC_BATCH, H_C, DK_C), 1.0),
        'state_mlstm_m': nrm((DEPTH, DEC_BATCH, H_C), 1.0),
        'state_mlstm_conv': nrm((DEPTH, DEC_BATCH, CONV_W - 1, 2 * W_C), 1.0),
        'meta_tokens': nrm((N_META, D_MODEL), 1.0),
        'ln_emb_g': 1.0 + nrm((D_MODEL,), 0.02),
        'ln_emb_b': nrm((D_MODEL,), 0.02),
        'w_in': nrm((DEPTH, D_MODEL, N_PROJ), D_MODEL ** -0.5) * col_scale,
        'b_in': nrm((DEPTH, N_PROJ), 0.02),
        'b_mlstm_f': jnp.linspace(3.0, 6.0, H_C)[None, :] + nrm((DEPTH, H_C), 0.1),
        'conv_w': nrm((DEPTH, CONV_W, 2 * W_C), 0.5),
        'conv_b': nrm((DEPTH, 2 * W_C), 0.02),
        'hgrn_lb_logits': nrm((DEPTH, H_A * DK_A), 1.0),
        'lambda_q1': nrm((DEPTH, DH_B), 0.1),
        'lambda_k1': nrm((DEPTH, DH_B), 0.1),
        'lambda_q2': nrm((DEPTH, DH_B), 0.1),
        'lambda_k2': nrm((DEPTH, DH_B), 0.1),
        'norm_a': 1.0 + nrm((DEPTH, W_A), 0.02),
        'norm_b': 1.0 + nrm((DEPTH, W_B), 0.02),
        'norm_c': 1.0 + nrm((DEPTH, W_C), 0.02),
        'w_out': nrm((DEPTH, MIX_WIDTH, D_MODEL), MIX_WIDTH ** -0.5) * BETA,
        'ln1_g': 1.0 + nrm((DEPTH, D_MODEL), 0.02),
        'ln1_b': nrm((DEPTH, D_MODEL), 0.02),
        'w_router_group': nrm((DEPTH, D_MODEL, N_GROUPS), D_MODEL ** -0.5),
        'b_router_group': nrm((DEPTH, N_GROUPS), 0.01),
        'w_router_expert': nrm((DEPTH, D_MODEL, N_EXPERTS), D_MODEL ** -0.5),
        'b_router_expert': nrm((DEPTH, N_EXPERTS), 0.01),
        'w_exp_gate': nrm((DEPTH, N_EXPERTS, D_MODEL, D_EXPERT), D_MODEL ** -0.5),
        'w_exp_up': nrm((DEPTH, N_EXPERTS, D_MODEL, D_EXPERT), D_MODEL ** -0.5),
        'w_exp_down': nrm((DEPTH, N_EXPERTS, D_EXPERT, D_MODEL), D_EXPERT ** -0.5) * BETA,
        'ln2_g': 1.0 + nrm((DEPTH, D_MODEL), 0.02),
        'ln2_b': nrm((DEPTH, D_MODEL), 0.02),
    }


def reference(x_prompt, x_sample, cache_k, cache_v, page_table, state_hgrn, state_mlstm_c, state_mlstm_n,
              state_mlstm_m, state_mlstm_conv, meta_tokens, ln_emb_g, ln_emb_b, w_in, b_in, b_mlstm_f,
              conv_w, conv_b, hgrn_lb_logits, lambda_q1, lambda_k1, lambda_q2, lambda_k2, norm_a, norm_b,
              norm_c, w_out, ln1_g, ln1_b, w_router_group, b_router_group, w_router_expert, b_router_expert,
              w_exp_gate, w_exp_up, w_exp_down, ln2_g, ln2_b):
    f32 = jnp.float32
    n_meta = meta_tokens.shape[0]
    Bp, Lq, _ = x_prompt.shape
    Bs, Ls, _ = x_sample.shape
    meta = jnp.broadcast_to(meta_tokens[None].astype(x_prompt.dtype), (Bp, n_meta, D_MODEL))
    hp = _layernorm(jnp.concatenate([meta, x_prompt], axis=1), ln_emb_g, ln_emb_b)
    hs = _layernorm(x_sample, ln_emb_g, ln_emb_b)
    prompt_segs = (n_meta, Lq)
    sample_segs = (Ls,)

    lb_sm = jax.nn.softmax(hgrn_lb_logits.astype(f32), axis=0)
    lb_all = jnp.cumsum(lb_sm, axis=0) - lb_sm[0]

    kp_l, vp_l, ks_l, vs_l = [], [], [], []
    sa_p_l, sa_s_l, cc_p_l, cc_s_l, nc_p_l, nc_s_l, mc_p_l, mc_s_l, cv_p_l, cv_s_l = ([] for _ in range(10))
    for l in range(DEPTH):
        p = {'w_in': w_in[l], 'b_in': b_in[l], 'b_mlstm_f': b_mlstm_f[l], 'conv_w': conv_w[l],
             'conv_b': conv_b[l], 'norm_a': norm_a[l], 'norm_b': norm_b[l], 'norm_c': norm_c[l],
             'w_out': w_out[l], 'ln1_g': ln1_g[l], 'ln1_b': ln1_b[l], 'w_rg': w_router_group[l],
             'b_rg': b_router_group[l], 'w_re': w_router_expert[l], 'b_re': b_router_expert[l],
             'w_eg': w_exp_gate[l], 'w_eu': w_exp_up[l], 'w_ed': w_exp_down[l],
             'ln2_g': ln2_g[l], 'ln2_b': ln2_b[l]}
        lam_init = 0.8 - 0.6 * math.exp(-0.3 * l)
        lam = (jnp.exp(jnp.sum(lambda_q1[l].astype(f32) * lambda_k1[l].astype(f32)))
               - jnp.exp(jnp.sum(lambda_q2[l].astype(f32) * lambda_k2[l].astype(f32))) + lam_init)

        hp, (kp, vp, sa_p, cc_p, nc_p, mc_p, cv_p) = _layer(
            hp, p, lb_all[l], lam, lam_init, None, prompt_segs,
            jnp.zeros((Bp, H_A, DK_A, DV_A), f32), jnp.zeros((Bp, H_C, DV_C, DK_C), f32),
            jnp.zeros((Bp, H_C, DK_C), f32), jnp.zeros((Bp, H_C), f32),
            jnp.zeros((Bp, CONV_W - 1, 2 * W_C), x_prompt.dtype))

        k_past = cache_k[l, page_table].reshape(Bs, -1, H_B, 2 * DH_B)
        v_past = cache_v[l, page_table].reshape(Bs, -1, H_B, DV_B)
        hs, (ksn, vsn, sa_s, cc_s, nc_s, mc_s, cv_s) = _layer(
            hs, p, lb_all[l], lam, lam_init, (k_past, v_past), sample_segs,
            state_hgrn[l], state_mlstm_c[l], state_mlstm_n[l], state_mlstm_m[l], state_mlstm_conv[l])

        kp_l.append(kp); vp_l.append(vp); ks_l.append(ksn); vs_l.append(vsn)
        sa_p_l.append(sa_p); sa_s_l.append(sa_s); cc_p_l.append(cc_p); cc_s_l.append(cc_s)
        nc_p_l.append(nc_p); nc_s_l.append(nc_s); mc_p_l.append(mc_p); mc_s_l.append(mc_s)
        cv_p_l.append(cv_p); cv_s_l.append(cv_s)

    y_prompt = hp[:, n_meta:]
    return (y_prompt, hs,
            jnp.stack(kp_l), jnp.stack(vp_l), jnp.stack(ks_l), jnp.stack(vs_l),
            jnp.stack(sa_p_l), jnp.stack(sa_s_l),
            jnp.stack(cc_p_l), jnp.stack(cc_s_l),
            jnp.stack(nc_p_l), jnp.stack(nc_s_l),
            jnp.stack(mc_p_l), jnp.stack(mc_s_l),
            jnp.stack(cv_p_l), jnp.stack(cv_s_l))
```

```python
import functools
import math

import jax
import jax.numpy as jnp
from jax import lax
from jax.experimental import pallas as pl
from jax.experimental.pallas import tpu as pltpu

F32 = jnp.float32
BF16 = jnp.bfloat16
HIGHEST = lax.Precision.HIGHEST

N_META = 16
N_HEADS = 4
HEAD_W = 64
ATT_W = 2 * HEAD_W
REC_W = N_HEADS * HEAD_W
ATT_ALL = N_HEADS * ATT_W
CONV_TAPS = 4
N_GROUPS = 4
GROUP_EXPERTS = 8
N_EXPERTS = N_GROUPS * GROUP_EXPERTS
ROUTE_LANES = 128
EXPERT_LANE0 = N_GROUPS
MOE_ROWS = 128
LN_EPS = 1e-5
NORM_EPS = 1e-6
NEG = -1e30

SEQ_TILE = 128
REC_CHUNK = 128
SUB = 8
ATT_TILE = 384
PAGES_PER_STEP = 8
ROW_TILE = 256
VMEM_LIMIT = 48 * 1024 * 1024

IN_WIDTHS = (4 * REC_W, ATT_ALL, ATT_ALL, ATT_ALL, 4 * REC_W, ROUTE_LANES)
N_PROJ_RAW = 4 * REC_W + 3 * ATT_ALL + 4 * REC_W + 2 * N_HEADS
N_PROJ_PAD = sum(IN_WIDTHS)


def _nt(a, b):
    return lax.dot_general(a, b, (((1,), (1,)), ((), ())), preferred_element_type=F32)


def _tn(a, b):
    return lax.dot_general(a, b, (((0,), (0,)), ((), ())), preferred_element_type=F32)


def _dot(a, b, precision=None):
    return jnp.dot(a, b, preferred_element_type=F32, precision=precision)


def _layernorm(x, g, b):
    mu = jnp.mean(x, axis=-1, keepdims=True)
    xc = x - mu
    var = jnp.mean(xc * xc, axis=-1, keepdims=True)
    return xc * lax.rsqrt(var + LN_EPS) * g + b


def _sigmoid(x):
    return 1.0 / (1.0 + jnp.exp(-x))


def _silu(x):
    return x * _sigmoid(x)


def _head_lane_masks(width):
    lane = lax.broadcasted_iota(jnp.int32, (1, width), 1)
    per = width // N_HEADS
    return [(lane // per == h).astype(F32) for h in range(N_HEADS)]


def _block_diag_mask(n, blk):
    r = lax.broadcasted_iota(jnp.int32, (n, n), 0) // blk
    c = lax.broadcasted_iota(jnp.int32, (n, n), 1) // blk
    return r == c


def _tril_ones(n):
    r = lax.broadcasted_iota(jnp.int32, (n, n), 0)
    c = lax.broadcasted_iota(jnp.int32, (n, n), 1)
    return (r >= c).astype(F32)


def _params(*sem):
    return pltpu.CompilerParams(dimension_semantics=sem, vmem_limit_bytes=VMEM_LIMIT)


def _inproj_kernel(do_ln, x_ref, g_ref, b_ref, w_ref, bias_ref, *outs):
    x = x_ref[...]
    if do_ln:
        x = _layernorm(x, g_ref[...], b_ref[...])
        outs[-1][...] = x
    xb = x.astype(BF16)
    col = 0
    for idx, width in enumerate(IN_WIDTHS):
        acc = _dot(xb, w_ref[:, col:col + width]) + bias_ref[:, col:col + width]
        if idx == 1:
            acc = acc * (HEAD_W ** -0.5)
        outs[idx][...] = acc.astype(outs[idx].dtype)
        col += width


def _inproj(x, ln_g, ln_b, w, bias, do_ln):
    T, D = x.shape
    tm = min(ROW_TILE, T)
    dtypes = (F32, BF16, F32, F32, F32, F32)
    out_shape = [jax.ShapeDtypeStruct((T, wd), dt) for wd, dt in zip(IN_WIDTHS, dtypes)]
    out_specs = [pl.BlockSpec((tm, wd), lambda i: (i, 0)) for wd in IN_WIDTHS]
    if do_ln:
        out_shape.append(jax.ShapeDtypeStruct((T, D), F32))
        out_specs.append(pl.BlockSpec((tm, D), lambda i: (i, 0)))
    row = lambda n: pl.BlockSpec((1, n), lambda i: (0, 0))
    return pl.pallas_call(
        functools.partial(_inproj_kernel, do_ln),
        grid=(T // tm,),
        in_specs=[pl.BlockSpec((tm, D), lambda i: (i, 0)), row(D), row(D),
                  pl.BlockSpec((D, N_PROJ_PAD), lambda i: (0, 0)), row(N_PROJ_PAD)],
        out_specs=out_specs,
        out_shape=out_shape,
        compiler_params=_params("parallel"),
        name="inproj",
    )(x, ln_g, ln_b, w, bias)


def _pad_rows(a, rows):
    if a.shape[0] == rows:
        return a
    return jnp.concatenate([a, jnp.zeros((rows - a.shape[0], a.shape[1]), a.dtype)], axis=0)


def _hgrn_kernel(C, Cb, l_real, a_ref, lb_ref, gain_ref, s0_ref, o_ref, sout_ref, st_ref):
    c = pl.program_id(1)

    @pl.when(c == 0)
    def _():
        st_ref[...] = s0_ref[...]

    W = REC_W
    a = _pad_rows(a_ref[...], C)
    q, fa, v, ga = a[:, 0:W], a[:, W:2 * W], a[:, 2 * W:3 * W], a[:, 3 * W:4 * W]
    lb = lb_ref[...]
    valid = (c * C + lax.broadcasted_iota(jnp.int32, (C, 1), 0)) < l_real
    f = lb + (1.0 - lb) * _sigmoid(fa)
    g = jnp.where(valid, jnp.log(f), 0.0)
    kk = jnp.where(valid, 1.0 - f, 0.0)
    b = _dot(_tril_ones(C), g, HIGHEST)

    bd = _block_diag_mask(W, HEAD_W)
    bd_bf = bd.astype(BF16)
    hm = _head_lane_masks(W)

    nb = C // SUB
    b3, q3, k3, v3 = (z.reshape(nb, SUB, W) for z in (b, q, kk, v))
    t_in = lax.broadcasted_iota(jnp.int32, (nb, SUB, W), 1)
    o3 = jnp.zeros((nb, SUB, W), F32)
    for s in range(SUB):
        e = jnp.exp(jnp.minimum(b3 - b3[:, s:s + 1, :], 0.0))
        p = jnp.where(t_in >= s, e * q3 * k3[:, s:s + 1, :], 0.0)
        r = _dot(p.reshape(C, W).astype(BF16), bd_bf)
        o3 = o3 + r.reshape(nb, SUB, W) * v3[:, s:s + 1, :]
    o = o3.reshape(C, W)

    t_row = lax.broadcasted_iota(jnp.int32, (C, 1), 0)
    s_col = lax.broadcasted_iota(jnp.int32, (1, C), 1)
    att = [jnp.zeros((C, C), F32) for _ in range(N_HEADS)]
    m = SUB
    while m < C:
        blk = 2 * m
        ref = jnp.broadcast_to(b.reshape(C // blk, blk, W)[:, m - 1:m, :], (C // blk, blk, W)).reshape(C, W)
        later = (t_row % blk) >= m
        qt = jnp.where(later, q * jnp.exp(jnp.minimum(b - ref, 0.0)), 0.0)
        kt = jnp.where(later, 0.0, kk * jnp.exp(jnp.minimum(ref - b, 0.0))).astype(BF16)
        same = (t_row // blk) == (s_col // blk)
        for h in range(N_HEADS):
            sc = _nt((qt * hm[h]).astype(BF16), kt)
            att[h] = att[h] + jnp.where(same, sc, 0.0)
        m = blk
    if C > SUB:
        for h in range(N_HEADS):
            o = o + _dot(att[h].astype(BF16), (v * hm[h]).astype(BF16))

    st = st_ref[...]
    o = o + _nt((q * jnp.exp(b)).astype(BF16), st.astype(BF16))
    b_end = b[C - 1:C, :]
    ke = kk * jnp.exp(b_end - b)
    st_new = st * jnp.exp(b_end) + jnp.where(bd, _tn(v.astype(BF16), ke.astype(BF16)), 0.0)
    st_ref[...] = st_new
    sout_ref[...] = st_new

    ms = _dot(o * o, bd.astype(F32), HIGHEST) * (1.0 / HEAD_W)
    y = o * lax.rsqrt(ms + NORM_EPS) * gain_ref[...] * _silu(ga)
    o_ref[...] = y[:Cb].astype(o_ref.dtype)


def _hgrn(a, lb, gain, s0, l_real):
    B, L, _ = a.shape
    Cb = min(REC_CHUNK, L)
    W = REC_W
    return pl.pallas_call(
        functools.partial(_hgrn_kernel, REC_CHUNK, Cb, l_real),
        grid=(B, L // Cb),
        in_specs=[pl.BlockSpec((None, Cb, 4 * W), lambda b, c: (b, c, 0)),
                  pl.BlockSpec((1, W), lambda b, c: (0, 0)),
                  pl.BlockSpec((1, W), lambda b, c: (0, 0)),
                  pl.BlockSpec((None, W, W), lambda b, c: (b, 0, 0))],
        out_specs=[pl.BlockSpec((None, Cb, W), lambda b, c: (b, c, 0)),
                   pl.BlockSpec((None, W, W), lambda b, c: (b, 0, 0))],
        out_shape=[jax.ShapeDtypeStruct((B, L, W), BF16), jax.ShapeDtypeStruct((B, W, W), F32)],
        scratch_shapes=[pltpu.VMEM((W, W), F32)],
        compiler_params=_params("parallel", "arbitrary"),
        name="hgrn",
    )(a, lb, gain, s0)


def _log_sigmoid(x):
    return jnp.minimum(x, 0.0) - jnp.log1p(jnp.exp(-jnp.abs(x)))


def _mlstm_kernel(C, Cb, l_real, cg_ref, gt_ref, cw_ref, cb_ref, bf_ref, gain_ref, c0_ref, n0_ref, m0_ref,
                  tail0_ref, o_ref, cout_ref, nout_ref, mout_ref, ct_ref, n_ref, m_ref, tail_ref):
    c = pl.program_id(1)

    @pl.when(c == 0)
    def _():
        ct_ref[...] = c0_ref[...]
        n_ref[...] = n0_ref[...]
        m_ref[...] = m0_ref[...]
        tail_ref[...] = tail0_ref[...]

    W = REC_W
    cg = _pad_rows(cg_ref[...], C)
    gt = _pad_rows(gt_ref[...], C)
    valid = (c * C + lax.broadcasted_iota(jnp.int32, (C, 1), 0)) < l_real

    qk_pre = cg[:, 0:2 * W]
    xp = jnp.concatenate([tail_ref[...], qk_pre], axis=0)
    conv = cb_ref[...]
    base = SUB - (CONV_TAPS - 1)
    for j in range(CONV_TAPS):
        conv = conv + cw_ref[j:j + 1, :] * xp[base + j:base + j + C, :]
    tail_ref[...] = qk_pre[C - SUB:C, :]
    qk = _silu(conv)
    qc = qk[:, 0:W]
    kc = qk[:, W:2 * W] * (HEAD_W ** -0.5)
    vc = cg[:, 2 * W:3 * W]
    oc = cg[:, 3 * W:4 * W]

    lf = jnp.where(valid, _log_sigmoid(gt + bf_ref[...]), 0.0)
    li = jnp.where(valid, gt, NEG)
    fcum = _dot(_tril_ones(C), lf, HIGHEST)
    lane_r = lax.broadcasted_iota(jnp.int32, (ROUTE_LANES, W), 0)
    lane_c = lax.broadcasted_iota(jnp.int32, (ROUTE_LANES, W), 1) // HEAD_W
    li_b = _dot(li, (lane_r == lane_c).astype(F32), HIGHEST)
    f_b = _dot(fcum, (lane_r == lane_c + N_HEADS).astype(F32), HIGHEST)
    f_t = fcum.T
    li_t = li.T

    hm = _head_lane_masks(W)
    bd = _block_diag_mask(W, HEAD_W)
    causal = lax.broadcasted_iota(jnp.int32, (C, C), 0) >= lax.broadcasted_iota(jnp.int32, (C, C), 1)
    m_prev = m_ref[...]
    kc_bf = kc.astype(BF16)
    num = jnp.zeros((C, W), F32)
    den = jnp.zeros((C, W), F32)
    cs_b = jnp.zeros((C, W), F32)
    mt_b = jnp.zeros((C, W), F32)
    for h in range(N_HEADS):
        f_col = fcum[:, N_HEADS + h:N_HEADS + h + 1]
        log_d = jnp.where(causal, f_col - f_t[N_HEADS + h:N_HEADS + h + 1, :] + li_t[h:h + 1, :], NEG)
        log_s = f_col + m_prev[:, h * HEAD_W:h * HEAD_W + 1]
        m_t = jnp.maximum(jnp.max(log_d, axis=1, keepdims=True), log_s)
        w = _nt((qc * hm[h]).astype(BF16), kc_bf) * jnp.exp(log_d - m_t)
        num = num + _dot(w.astype(BF16), (vc * hm[h]).astype(BF16))
        den = den + jnp.sum(w, axis=1, keepdims=True) * hm[h]
        cs_b = cs_b + jnp.exp(log_s - m_t) * hm[h]
        mt_b = mt_b + m_t * hm[h]

    ct = ct_ref[...]
    n_row = n_ref[...]
    num = num + cs_b * _nt(qc.astype(BF16), ct.astype(BF16))
    den = den + cs_b * _dot((qc * n_row).astype(BF16), bd.astype(BF16))
    hh = num / jnp.maximum(jnp.abs(den), jnp.exp(-mt_b))

    m_new = mt_b[C - 1:C, :]
    f_end = f_b[C - 1:C, :]
    ws = jnp.exp(f_end - f_b + li_b - m_new)
    decay = jnp.exp(f_end + m_prev - m_new)
    ct_new = ct * decay + jnp.where(bd, _tn((vc * ws).astype(BF16), kc_bf), 0.0)
    n_new = decay * n_row + jnp.sum(ws * kc, axis=0, keepdims=True)
    ct_ref[...] = ct_new
    n_ref[...] = n_new
    m_ref[...] = m_new
    cout_ref[...] = ct_new
    nout_ref[...] = n_new
    mout_ref[...] = m_new

    ms = _dot(hh * hh, bd.astype(F32), HIGHEST) * (1.0 / HEAD_W)
    y = _sigmoid(oc) * (hh * lax.rsqrt(ms + NORM_EPS) * gain_ref[...])
    o_ref[...] = y[:Cb].astype(o_ref.dtype)


def _mlstm(cg, gates, conv_w, conv_b, bias_f, gain, c0, n0, m0, tail0, l_real):
    B, L, _ = cg.shape
    Cb = min(REC_CHUNK, L)
    W = REC_W
    const = lambda shape: pl.BlockSpec(shape, lambda b, c: (0,) * len(shape))
    per_b = lambda shape: pl.BlockSpec((None,) + shape, lambda b, c: (b,) + (0,) * len(shape))
    return pl.pallas_call(
        functools.partial(_mlstm_kernel, REC_CHUNK, Cb, l_real),
        grid=(B, L // Cb),
        in_specs=[pl.BlockSpec((None, Cb, 4 * W), lambda b, c: (b, c, 0)),
                  pl.BlockSpec((None, Cb, ROUTE_LANES), lambda b, c: (b, c, 0)),
                  const((CONV_TAPS, 2 * W)), const((1, 2 * W)), const((1, ROUTE_LANES)), const((1, W)),
                  per_b((W, W)), per_b((1, W)), per_b((1, W)), per_b((SUB, 2 * W))],
        out_specs=[pl.BlockSpec((None, Cb, W), lambda b, c: (b, c, 0)),
                   per_b((W, W)), per_b((1, W)), per_b((1, W))],
        out_shape=[jax.ShapeDtypeStruct((B, L, W), BF16), jax.ShapeDtypeStruct((B, W, W), F32),
                   jax.ShapeDtypeStruct((B, 1, W), F32), jax.ShapeDtypeStruct((B, 1, W), F32)],
        scratch_shapes=[pltpu.VMEM((W, W), F32), pltpu.VMEM((1, W), F32), pltpu.VMEM((1, W), F32),
                        pltpu.VMEM((SUB, 2 * W), F32)],
        compiler_params=_params("parallel", "arbitrary"),
        name="mlstm",
    )(cg, gates, conv_w, conv_b, bias_f, gain, c0, n0, m0, tail0)


def _diff_finish(o1, o2, lam, gain, out_scale):
    o = o1 - lam * o2
    ms = jnp.mean(o * o, axis=-1, keepdims=True)
    return out_scale * (o * lax.rsqrt(ms + NORM_EPS) * gain)


def _attn_kernel(T, out_scale, q_ref, k_ref, v_ref, lam_ref, gain_ref, o_ref, m_sc, l_sc, acc_sc):
    qi = pl.program_id(2)
    ki = pl.program_id(3)

    @pl.when(ki == 0)
    def _():
        m_sc[...] = jnp.full(m_sc.shape, NEG, F32)
        l_sc[...] = jnp.zeros(l_sc.shape, F32)
        acc_sc[...] = jnp.zeros(acc_sc.shape, F32)

    @pl.when(ki <= qi)
    def _():
        q = q_ref[...]
        lane = lax.broadcasted_iota(jnp.int32, (1, ATT_W), 1)
        zero = jnp.zeros_like(q)
        qq = jnp.concatenate([jnp.where(lane < HEAD_W, q, zero), jnp.where(lane >= HEAD_W, q, zero)], axis=0)
        s = _nt(qq, k_ref[...].astype(BF16))
        q_pos = qi * T + lax.broadcasted_iota(jnp.int32, (2 * T, 1), 0) % T
        k_pos = ki * T + lax.broadcasted_iota(jnp.int32, (1, T), 1)
        s = jnp.where(q_pos >= k_pos, s, NEG)
        m_old = m_sc[...]
        m_new = jnp.maximum(m_old, jnp.max(s, axis=1, keepdims=True))
        alpha = jnp.exp(m_old - m_new)
        p = jnp.exp(s - m_new)
        l_sc[...] = alpha * l_sc[...] + jnp.sum(p, axis=1, keepdims=True)
        acc_sc[...] = alpha * acc_sc[...] + _dot(p.astype(BF16), v_ref[...].astype(BF16))
        m_sc[...] = m_new

    @pl.when(ki == qi)
    def _():
        on = acc_sc[...] / l_sc[...]
        y = _diff_finish(on[:T], on[T:], lam_ref[...], gain_ref[...], out_scale)
        o_ref[...] = y.astype(o_ref.dtype)


def _attn_prompt(q, k, v, lam_row, gain, out_scale):
    B, L, _ = q.shape
    T = ATT_TILE if L % ATT_TILE == 0 else SEQ_TILE
    n = L // T
    kv_spec = pl.BlockSpec((None, T, ATT_W), lambda b, h, qi, ki: (b, jnp.minimum(ki, qi), h))
    return pl.pallas_call(
        functools.partial(_attn_kernel, T, out_scale),
        grid=(B, N_HEADS, n, n),
        in_specs=[pl.BlockSpec((None, T, ATT_W), lambda b, h, qi, ki: (b, qi, h)), kv_spec, kv_spec,
                  pl.BlockSpec((1, ATT_W), lambda b, h, qi, ki: (0, 0)),
                  pl.BlockSpec((1, ATT_W), lambda b, h, qi, ki: (0, h))],
        out_specs=pl.BlockSpec((None, T, ATT_W), lambda b, h, qi, ki: (b, qi, h)),
        out_shape=jax.ShapeDtypeStruct((B, L, ATT_ALL), BF16),
        scratch_shapes=[pltpu.VMEM((2 * T, 1), F32), pltpu.VMEM((2 * T, 1), F32), pltpu.VMEM((2 * T, ATT_W), F32)],
        compiler_params=_params("parallel", "parallel", "parallel", "arbitrary"),
        name="attn_prompt",
    )(q, k, v, lam_row, gain)


def _dec_attn_kernel(npg, ls, page, out_scale, pt_ref, q_ref, kn_ref, vn_ref, lam_ref, gain_ref, *refs):
    del pt_ref
    k_refs, v_refs = refs[:npg], refs[npg:2 * npg]
    o_ref, m_sc, l_sc, acc_sc = refs[2 * npg:]
    j = pl.program_id(1)
    rows = 2 * N_HEADS * ls

    @pl.when(j == 0)
    def _():
        m_sc[...] = jnp.full(m_sc.shape, NEG, F32)
        l_sc[...] = jnp.zeros(l_sc.shape, F32)
        acc_sc[...] = jnp.zeros(acc_sc.shape, F32)

    q = q_ref[...].astype(F32)
    lane = lax.broadcasted_iota(jnp.int32, (1, ATT_ALL), 1)
    qbd = jnp.concatenate([jnp.where(lane // HEAD_W == hc, q, 0.0) for hc in range(2 * N_HEADS)],
                          axis=0).astype(BF16)

    def update(s, values):
        m_old = m_sc[...]
        m_new = jnp.maximum(m_old, jnp.max(s, axis=1, keepdims=True))
        alpha = jnp.exp(m_old - m_new)
        p = jnp.exp(s - m_new)
        l_sc[...] = alpha * l_sc[...] + jnp.sum(p, axis=1, keepdims=True)
        pv = jnp.zeros(acc_sc.shape, F32)
        for i, val in enumerate(values):
            pv = pv + _dot(p[:, i * page:(i + 1) * page].astype(BF16), val)
        acc_sc[...] = alpha * acc_sc[...] + pv
        m_sc[...] = m_new

    s_past = jnp.concatenate([_nt(qbd, kr[...].astype(BF16)) for kr in k_refs], axis=1)
    update(s_past, [vr[...].astype(BF16) for vr in v_refs])

    @pl.when(j == pl.num_programs(1) - 1)
    def _():
        kn = _pad_rows(kn_ref[...], page).astype(BF16)
        vn = _pad_rows(vn_ref[...], page).astype(BF16)
        s = _nt(qbd, kn)
        q_idx = lax.broadcasted_iota(jnp.int32, (rows, 1), 0) % ls
        k_idx = lax.broadcasted_iota(jnp.int32, (1, page), 1)
        s = jnp.where(k_idx <= q_idx, s, NEG)
        update(s, [vn])
        on = acc_sc[...] / l_sc[...]
        r = lax.broadcasted_iota(jnp.int32, (rows, 1), 0)
        coef = jnp.where((r // ls) % 2 == 0, 1.0, -lam_ref[...][:, 0:1])
        z = jnp.where(r // (2 * ls) == lane // ATT_W, on * coef, 0.0)
        o = jnp.sum(z.reshape(2 * N_HEADS, ls, ATT_ALL), axis=0)
        gain = gain_ref[...]
        parts = []
        for h in range(N_HEADS):
            oh = o[:, h * ATT_W:(h + 1) * ATT_W]
            ms = jnp.mean(oh * oh, axis=-1, keepdims=True)
            parts.append(out_scale * (oh * lax.rsqrt(ms + NORM_EPS) * gain[:, h * ATT_W:(h + 1) * ATT_W]))
        o_ref[...] = jnp.concatenate(parts, axis=1).astype(o_ref.dtype)


def _attn_sample(q, k_new, v_new, cache_k, cache_v, page_table, layer, lam_row, gain, out_scale):
    B, ls, _ = q.shape
    n_pages = page_table.shape[1]
    page = cache_k.shape[2]
    npg = PAGES_PER_STEP if n_pages % PAGES_PER_STEP == 0 else 1
    ck = cache_k.reshape(cache_k.shape[0], cache_k.shape[1], page, ATT_ALL)
    cv = cache_v.reshape(cache_v.shape[0], cache_v.shape[1], page, ATT_ALL)

    def page_spec(i):
        return pl.BlockSpec((None, None, page, ATT_ALL), lambda b, j, pt: (layer, pt[b, j * npg + i], 0, 0))

    per_b = pl.BlockSpec((None, ls, ATT_ALL), lambda b, j, pt: (b, 0, 0))
    rows = 2 * N_HEADS * ls
    grid_spec = pltpu.PrefetchScalarGridSpec(
        num_scalar_prefetch=1,
        grid=(B, n_pages // npg),
        in_specs=[per_b, per_b, per_b,
                  pl.BlockSpec((1, ATT_W), lambda b, j, pt: (0, 0)),
                  pl.BlockSpec((1, ATT_ALL), lambda b, j, pt: (0, 0))]
                 + [page_spec(i) for i in range(npg)] * 2,
        out_specs=per_b,
        scratch_shapes=[pltpu.VMEM((rows, 1), F32), pltpu.VMEM((rows, 1), F32), pltpu.VMEM((rows, ATT_ALL), F32)],
    )
    return pl.pallas_call(
        functools.partial(_dec_attn_kernel, npg, ls, page, out_scale),
        grid_spec=grid_spec,
        out_shape=jax.ShapeDtypeStruct((B, ls, ATT_ALL), BF16),
        compiler_params=_params("parallel", "arbitrary"),
        name="attn_sample",
    )(page_table, q, k_new, v_new, lam_row, gain, *([ck] * npg), *([cv] * npg))


def _outproj_kernel(alpha, ma_ref, mb_ref, mc_ref, h_ref, wo_ref, g_ref, b_ref, wr_ref, br_ref,
                    h1_ref, eid_ref, gate_ref):
    W = REC_W
    mix = (_dot(ma_ref[...], wo_ref[0:W, :]) + _dot(mb_ref[...], wo_ref[W:W + ATT_ALL, :])
           + _dot(mc_ref[...], wo_ref[W + ATT_ALL:2 * W + ATT_ALL, :]))
    h1 = _layernorm(alpha * h_ref[...] + mix, g_ref[...], b_ref[...])
    h1_ref[...] = h1

    logits = _dot(h1.astype(BF16), wr_ref[...]) + br_ref[...]
    lane = lax.broadcasted_iota(jnp.int32, (1, ROUTE_LANES), 1)
    lane_f = lane.astype(F32)
    far = float(ROUTE_LANES)

    gl = jnp.where(lane < N_GROUPS, logits, NEG)
    g_max = jnp.max(gl, axis=1, keepdims=True)
    g_top = 1.0 / jnp.sum(jnp.exp(gl - g_max), axis=1, keepdims=True)
    g_idx = jnp.min(jnp.where(gl == g_max, lane_f, far), axis=1, keepdims=True)

    e_lane = lane_f - float(EXPERT_LANE0)
    in_group = (e_lane >= g_idx * GROUP_EXPERTS) & (e_lane < (g_idx + 1.0) * GROUP_EXPERTS)
    el = jnp.where(in_group, logits, NEG)
    e_max = jnp.max(el, axis=1, keepdims=True)
    e_sum = jnp.sum(jnp.exp(el - e_max), axis=1, keepdims=True)
    idx1 = jnp.min(jnp.where(el == e_max, lane_f, far), axis=1, keepdims=True)
    el2 = jnp.where(lane_f == idx1, NEG, el)
    e_max2 = jnp.max(el2, axis=1, keepdims=True)
    idx2 = jnp.min(jnp.where(el2 == e_max2, lane_f, far), axis=1, keepdims=True)
    p1 = 1.0 / e_sum
    p2 = jnp.exp(e_max2 - e_max) / e_sum
    tot = p1 + p2
    gate_ref[...] = jnp.where(lane == 0, g_top * (p1 / tot), jnp.where(lane == 1, g_top * (p2 / tot), 0.0))
    eid = jnp.where(lane == 0, idx1 - float(EXPERT_LANE0), jnp.where(lane == 1, idx2 - float(EXPERT_LANE0), 0.0))
    eid_ref[...] = eid.astype(jnp.int32)


def _outproj(ma, mb, mc, h, wo, g, b, wr, br, alpha):
    T, D = h.shape
    tm = min(ROW_TILE, T)
    tile = lambda n: pl.BlockSpec((tm, n), lambda i: (i, 0))
    const = lambda r, n: pl.BlockSpec((r, n), lambda i: (0, 0))
    return pl.pallas_call(
        functools.partial(_outproj_kernel, alpha),
        grid=(T // tm,),
        in_specs=[tile(REC_W), tile(ATT_ALL), tile(REC_W), tile(D), const(2 * REC_W + ATT_ALL, D),
                  const(1, D), const(1, D), const(D, ROUTE_LANES), const(1, ROUTE_LANES)],
        out_specs=[tile(D), tile(ROUTE_LANES), tile(ROUTE_LANES)],
        out_shape=[jax.ShapeDtypeStruct((T, D), F32), jax.ShapeDtypeStruct((T, ROUTE_LANES), jnp.int32),
                   jax.ShapeDtypeStruct((T, ROUTE_LANES), F32)],
        compiler_params=_params("parallel"),
        name="outproj",
    )(ma, mb, mc, h, wo, g, b, wr, br)


def _expert_kernel(T, row_ref, blke_ref, live_ref, x_hbm, wg_ref, wu_ref, wd_ref, out_hbm,
                   xbuf, ybuf, gsem, ssem):
    del blke_ref
    i = pl.program_id(0)
    nblk = pl.num_programs(0)
    cur = i % 2

    def start_gather(blk, buf):
        def body(r, carry):
            tok = jnp.minimum(row_ref[blk, r] // 2, T - 1)
            pltpu.make_async_copy(x_hbm.at[pl.ds(tok, 1), :], xbuf.at[buf, pl.ds(r, 1), :], gsem.at[buf]).start()
            return carry
        lax.fori_loop(0, MOE_ROWS, body, 0)

    def scatter(blk, buf, wait):
        def body(r, carry):
            code = row_ref[blk, r]
            tok = code // 2
            slot = code % 2

            @pl.when(tok < T)
            def _():
                cp = pltpu.make_async_copy(ybuf.at[buf, pl.ds(r, 1), :], out_hbm.at[slot, pl.ds(tok, 1), :],
                                           ssem.at[buf])
                if wait:
                    cp.wait()
                else:
                    cp.start()
            return carry
        lax.fori_loop(0, MOE_ROWS, body, 0)

    @pl.when(i == 0)
    def _():
        start_gather(0, 0)

    @pl.when(i + 1 < nblk)
    def _():
        start_gather(i + 1, 1 - cur)

    pltpu.make_async_copy(x_hbm.at[pl.ds(0, MOE_ROWS), :], xbuf.at[cur], gsem.at[cur]).wait()

    @pl.when(i >= 2)
    def _():
        scatter(i - 2, cur, True)

    @pl.when(live_ref[i] > 0)
    def _():
        xb = xbuf[cur].astype(BF16)
        hid = _silu(_dot(xb, wg_ref[...])) * _dot(xb, wu_ref[...])
        ybuf[cur] = _dot(hid.astype(BF16), wd_ref[...])

    scatter(i, cur, False)

    @pl.when(i == nblk - 1)
    def _():
        @pl.when(i >= 1)
        def _():
            scatter(i - 1, 1 - cur, True)
        scatter(i, cur, True)


def _experts(x, row_code, blk_e, blk_live, wg, wu, wd):
    T, D = x.shape
    n_blocks = blk_e.shape[0]
    de = wg.shape[2]
    grid_spec = pltpu.PrefetchScalarGridSpec(
        num_scalar_prefetch=3,
        grid=(n_blocks,),
        in_specs=[pl.BlockSpec(memory_space=pl.ANY),
                  pl.BlockSpec((None, D, de), lambda i, code, be, live: (be[i], 0, 0)),
                  pl.BlockSpec((None, D, de), lambda i, code, be, live: (be[i], 0, 0)),
                  pl.BlockSpec((None, de, D), lambda i, code, be, live: (be[i], 0, 0))],
        out_specs=pl.BlockSpec(memory_space=pl.ANY),
        scratch_shapes=[pltpu.VMEM((2, MOE_ROWS, D), F32), pltpu.VMEM((2, MOE_ROWS, D), F32),
                        pltpu.SemaphoreType.DMA((2,)), pltpu.SemaphoreType.DMA((2,))],
    )
    return pl.pallas_call(
        functools.partial(_expert_kernel, T),
        grid_spec=grid_spec,
        out_shape=jax.ShapeDtypeStruct((2, T, D), F32),
        compiler_params=_params("arbitrary"),
        name="experts",
    )(row_code, blk_e, blk_live, x, wg, wu, wd)


def _dispatch(eid, T):
    K = 2
    A = T * K
    flat_e = eid[:, :K].reshape(-1)
    order = jnp.argsort(flat_e).astype(jnp.int32)
    se = flat_e[order]
    counts = jnp.zeros((N_EXPERTS,), jnp.int32).at[flat_e].add(1)
    padded = (counts + MOE_ROWS - 1) // MOE_ROWS * MOE_ROWS
    pad_end = jnp.cumsum(padded)
    pad_start = pad_end - padded
    start = jnp.cumsum(counts) - counts
    dest = pad_start[se] + jnp.arange(A, dtype=jnp.int32) - start[se]
    n_blocks = -(-A // MOE_ROWS) + N_EXPERTS
    R = n_blocks * MOE_ROWS
    row_code = jnp.full((R,), T * K, jnp.int32).at[dest].set(order).reshape(n_blocks, MOE_ROWS)
    blk_start = jnp.arange(n_blocks, dtype=jnp.int32) * MOE_ROWS
    blk_e = jnp.minimum(jnp.searchsorted(pad_end, blk_start, side='right'), N_EXPERTS - 1).astype(jnp.int32)
    blk_live = (blk_start < pad_end[-1]).astype(jnp.int32)
    return row_code, blk_e, blk_live


def _combine_kernel(alpha, h_ref, y_ref, gate_ref, g_ref, b_ref, o_ref):
    gate = gate_ref[...]
    y = gate[:, 0:1] * y_ref[0] + gate[:, 1:2] * y_ref[1]
    o_ref[...] = _layernorm(alpha * h_ref[...] + y, g_ref[...], b_ref[...])


def _combine(h1, slots, gates, g, b, alpha):
    T, D = h1.shape
    tm = min(ROW_TILE, T)
    return pl.pallas_call(
        functools.partial(_combine_kernel, alpha),
        grid=(T // tm,),
        in_specs=[pl.BlockSpec((tm, D), lambda i: (i, 0)), pl.BlockSpec((2, tm, D), lambda i: (0, i, 0)),
                  pl.BlockSpec((tm, ROUTE_LANES), lambda i: (i, 0)),
                  pl.BlockSpec((1, D), lambda i: (0, 0)), pl.BlockSpec((1, D), lambda i: (0, 0))],
        out_specs=pl.BlockSpec((tm, D), lambda i: (i, 0)),
        out_shape=jax.ShapeDtypeStruct((T, D), F32),
        compiler_params=_params("parallel"),
        name="combine",
    )(h1, slots, gates, g, b)


def _block_diag(blocks):
    B, H, n, _ = blocks.shape
    eye = jnp.eye(H, dtype=blocks.dtype)
    return jnp.einsum('bhij,hg->bhigj', blocks, eye).reshape(B, H * n, H * n)


def _diag_blocks(mat, H):
    B, n = mat.shape[0], mat.shape[1] // H
    m5 = mat.reshape(B, H, n, H, n)
    return jnp.stack([m5[:, h, :, h, :] for h in range(H)], axis=1)


def _layer(x, do_ln, B, L, l_real, lw, state, past):
    D = x.shape[1]
    outs = _inproj(x, lw['ln_g'], lw['ln_b'], lw['w_in'], lw['b_in'], do_ln)
    a, qb, kb, vb, cg, gates = outs[:6]
    h = outs[6] if do_ln else x
    s3 = lambda z: z.reshape(B, L, z.shape[-1])

    mix_a, st_a = _hgrn(s3(a), lw['lb'], lw['norm_a'], state['hgrn'], l_real)
    mix_c, ct, n_c, m_c = _mlstm(s3(cg), s3(gates), lw['conv_w'], lw['conv_b'], lw['b_f'], lw['norm_c'],
                                 state['c'], state['n'], state['m'], state['tail'], l_real)
    if past is None:
        mix_b = _attn_prompt(s3(qb), s3(kb), s3(vb), lw['lam'], lw['norm_b'], lw['out_scale'])
    else:
        mix_b = _attn_sample(s3(qb), s3(kb), s3(vb), past[0], past[1], past[2], lw['layer'], lw['lam'],
                             lw['norm_b'], lw['out_scale'])

    flat = lambda z: z.reshape(B * L, z.shape[-1])
    h1, eid, gate = _outproj(flat(mix_a), flat(mix_b), flat(mix_c), h, lw['w_out'], lw['ln1_g'], lw['ln1_b'],
                             lw['w_route'], lw['b_route'], lw['alpha'])
    row_code, blk_e, blk_live = _dispatch(eid, B * L)
    slots = _experts(h1, row_code, blk_e, blk_live, lw['w_eg'], lw['w_eu'], lw['w_ed'])
    h2 = _combine(h1, slots, gate, lw['ln2_g'], lw['ln2_b'], lw['alpha'])

    new_state = {
        'k': s3(kb), 'v': s3(vb),
        'hgrn': jnp.swapaxes(_diag_blocks(st_a, N_HEADS), -1, -2),
        'c': _diag_blocks(ct, N_HEADS),
        'n': n_c.reshape(B, N_HEADS, HEAD_W),
        'm': m_c.reshape(B, N_HEADS, HEAD_W)[:, :, 0],
        'conv': s3(cg)[:, l_real - (CONV_TAPS - 1):l_real, :2 * REC_W],
    }
    return h2, new_state


def kernel(x_prompt, x_sample, cache_k, cache_v, page_table, state_hgrn, state_mlstm_c, state_mlstm_n, state_mlstm_m, state_mlstm_conv, meta_tokens, ln_emb_g, ln_emb_b, w_in, b_in, b_mlstm_f, conv_w, conv_b, hgrn_lb_logits, lambda_q1, lambda_k1, lambda_q2, lambda_k2, norm_a, norm_b, norm_c, w_out, ln1_g, ln1_b, w_router_group, b_router_group, w_router_expert, b_router_expert, w_exp_gate, w_exp_up, w_exp_down, ln2_g, ln2_b):
    depth = w_in.shape[0]
    Bp, Lq, D = x_prompt.shape
    Bs, Ls, _ = x_sample.shape
    n_meta = meta_tokens.shape[0]
    l_real = n_meta + Lq
    Lp = -(-l_real // SEQ_TILE) * SEQ_TILE
    alpha = (2.0 * depth) ** 0.25
    row = lambda z: z.reshape(1, -1).astype(F32)

    meta = jnp.broadcast_to(meta_tokens[None].astype(x_prompt.dtype), (Bp, n_meta, D))
    xp = jnp.concatenate([meta, x_prompt, jnp.zeros((Bp, Lp - l_real, D), x_prompt.dtype)], axis=1)
    hp = xp.reshape(Bp * Lp, D)
    hs = x_sample.reshape(Bs * Ls, D)

    lb_sm = jax.nn.softmax(hgrn_lb_logits.astype(F32), axis=0)
    lb_all = jnp.cumsum(lb_sm, axis=0) - lb_sm[0]

    zeros_p = {
        'hgrn': jnp.zeros((Bp, REC_W, REC_W), F32), 'c': jnp.zeros((Bp, REC_W, REC_W), F32),
        'n': jnp.zeros((Bp, 1, REC_W), F32), 'm': jnp.zeros((Bp, 1, REC_W), F32),
        'tail': jnp.zeros((Bp, SUB, 2 * REC_W), F32),
    }

    res_p, res_s = [], []
    for l in range(depth):
        lam_init = 0.8 - 0.6 * math.exp(-0.3 * l)
        lam = (jnp.exp(jnp.sum(lambda_q1[l].astype(F32) * lambda_k1[l].astype(F32)))
               - jnp.exp(jnp.sum(lambda_q2[l].astype(F32) * lambda_k2[l].astype(F32))) + lam_init)
        pad_cols = N_PROJ_PAD - N_PROJ_RAW
        bias_f = jnp.zeros((ROUTE_LANES,), F32).at[N_HEADS:2 * N_HEADS].set(b_mlstm_f[l].astype(F32))
        w_route = jnp.concatenate([w_router_group[l], w_router_expert[l]], axis=1)
        b_route = jnp.concatenate([b_router_group[l], b_router_expert[l]])
        r_pad = ROUTE_LANES - w_route.shape[1]
        lw = {
            'layer': l, 'alpha': alpha, 'out_scale': 1.0 - lam_init,
            'ln_g': row(ln_emb_g), 'ln_b': row(ln_emb_b),
            'w_in': jnp.pad(w_in[l], ((0, 0), (0, pad_cols))).astype(BF16),
            'b_in': row(jnp.pad(b_in[l], (0, pad_cols))),
            'lb': row(lb_all[l]), 'norm_a': row(norm_a[l]), 'norm_b': row(norm_b[l]), 'norm_c': row(norm_c[l]),
            'lam': jnp.full((1, ATT_W), lam, F32),
            'conv_w': conv_w[l].astype(F32), 'conv_b': row(conv_b[l]), 'b_f': row(bias_f),
            'w_out': w_out[l].astype(BF16), 'ln1_g': row(ln1_g[l]), 'ln1_b': row(ln1_b[l]),
            'w_route': jnp.pad(w_route, ((0, 0), (0, r_pad))).astype(BF16), 'b_route': row(jnp.pad(b_route, (0, r_pad))),
            'w_eg': w_exp_gate[l].astype(BF16), 'w_eu': w_exp_up[l].astype(BF16), 'w_ed': w_exp_down[l].astype(BF16),
            'ln2_g': row(ln2_g[l]), 'ln2_b': row(ln2_b[l]),
        }
        hp, st_p = _layer(hp, l == 0, Bp, Lp, l_real, lw, zeros_p, None)

        state_s = {
            'hgrn': _block_diag(jnp.swapaxes(state_hgrn[l].astype(F32), -1, -2)),
            'c': _block_diag(state_mlstm_c[l].astype(F32)),
            'n': state_mlstm_n[l].astype(F32).reshape(Bs, 1, REC_W),
            'm': jnp.repeat(state_mlstm_m[l].astype(F32), HEAD_W, axis=-1).reshape(Bs, 1, REC_W),
            'tail': jnp.pad(state_mlstm_conv[l].astype(F32), ((0, 0), (SUB - (CONV_TAPS - 1), 0), (0, 0))),
        }
        hs, st_s = _layer(hs, l == 0, Bs, Ls, Ls, lw, state_s, (cache_k, cache_v, page_table))
        res_p.append(st_p)
        res_s.append(st_s)

    y_prompt = hp.reshape(Bp, Lp, D)[:, n_meta:l_real]
    y_sample = hs.reshape(Bs, Ls, D)
    stack = lambda res, key, f=lambda z: z: jnp.stack([f(r[key]) for r in res])
    kv_p = lambda z: z[:, :l_real].reshape(Bp, l_real, N_HEADS, ATT_W)
    kv_s = lambda z: z.reshape(Bs, Ls, N_HEADS, ATT_W)
    return (y_prompt, y_sample,
            stack(res_p, 'k', kv_p), stack(res_p, 'v', kv_p), stack(res_s, 'k', kv_s), stack(res_s, 'v', kv_s),
            stack(res_p, 'hgrn'), stack(res_s, 'hgrn'),
            stack(res_p, 'c'), stack(res_s, 'c'),
            stack(res_p, 'n'), stack(res_s, 'n'),
            stack(res_p, 'm'), stack(res_s, 'm'),
            stack(res_p, 'conv'), stack(res_s, 'conv'))
```

```python
import functools
import math

import jax
import jax.numpy as jnp
from jax import lax
from jax.experimental import pallas as pl
from jax.experimental.pallas import tpu as pltpu

F32 = jnp.float32
BF16 = jnp.bfloat16
HIGHEST = lax.Precision.HIGHEST

N_META = 16
N_HEADS = 4
HEAD_W = 64
ATT_W = 2 * HEAD_W
REC_W = N_HEADS * HEAD_W
ATT_ALL = N_HEADS * ATT_W
CONV_TAPS = 4
N_GROUPS = 4
GROUP_EXPERTS = 8
N_EXPERTS = N_GROUPS * GROUP_EXPERTS
ROUTE_LANES = 128
EXPERT_LANE0 = N_GROUPS
MOE_ROWS = 128
LN_EPS = 1e-5
NORM_EPS = 1e-6
NEG = -1e30

SEQ_TILE = 128
REC_CHUNK = 128
SUB = 8
ATT_TILE = 384
PAGES_PER_STEP = 8
ROW_TILE = 256
VMEM_LIMIT = 48 * 1024 * 1024

IN_WIDTHS = (4 * REC_W, ATT_ALL, ATT_ALL, ATT_ALL, 4 * REC_W, ROUTE_LANES)
N_PROJ_RAW = 4 * REC_W + 3 * ATT_ALL + 4 * REC_W + 2 * N_HEADS
N_PROJ_PAD = sum(IN_WIDTHS)


def _nt(a, b):
    return lax.dot_general(a, b, (((1,), (1,)), ((), ())), preferred_element_type=F32)


def _tn(a, b):
    return lax.dot_general(a, b, (((0,), (0,)), ((), ())), preferred_element_type=F32)


def _dot(a, b, precision=None):
    return jnp.dot(a, b, preferred_element_type=F32, precision=precision)


def _layernorm(x, g, b):
    mu = jnp.mean(x, axis=-1, keepdims=True)
    xc = x - mu
    var = jnp.mean(xc * xc, axis=-1, keepdims=True)
    return xc * lax.rsqrt(var + LN_EPS) * g + b


def _sigmoid(x):
    return 1.0 / (1.0 + jnp.exp(-x))


def _silu(x):
    return x * _sigmoid(x)


def _head_lane_masks(width):
    lane = lax.broadcasted_iota(jnp.int32, (1, width), 1)
    per = width // N_HEADS
    return [(lane // per == h).astype(F32) for h in range(N_HEADS)]


def _block_diag_mask(n, blk):
    r = lax.broadcasted_iota(jnp.int32, (n, n), 0) // blk
    c = lax.broadcasted_iota(jnp.int32, (n, n), 1) // blk
    return r == c


def _tril_ones(n):
    r = lax.broadcasted_iota(jnp.int32, (n, n), 0)
    c = lax.broadcasted_iota(jnp.int32, (n, n), 1)
    return (r >= c).astype(F32)


def _params(*sem):
    return pltpu.CompilerParams(dimension_semantics=sem, vmem_limit_bytes=VMEM_LIMIT)


def _inproj_kernel(do_ln, x_ref, g_ref, b_ref, w_ref, bias_ref, *outs):
    x = x_ref[...]
    if do_ln:
        x = _layernorm(x, g_ref[...], b_ref[...])
        outs[-1][...] = x
    xb = x.astype(BF16)
    col = 0
    for idx, width in enumerate(IN_WIDTHS):
        acc = _dot(xb, w_ref[:, col:col + width]) + bias_ref[:, col:col + width]
        if idx == 1:
            acc = acc * (HEAD_W ** -0.5)
        outs[idx][...] = acc.astype(outs[idx].dtype)
        if idx in (2, 3):
            outs[len(IN_WIDTHS) + idx - 2][...] = acc.astype(BF16)
        col += width


def _inproj(x, ln_g, ln_b, w, bias, do_ln):
    T, D = x.shape
    tm = min(ROW_TILE, T)
    widths = IN_WIDTHS + (ATT_ALL, ATT_ALL)
    dtypes = (F32, BF16, F32, F32, F32, F32, BF16, BF16)
    out_shape = [jax.ShapeDtypeStruct((T, wd), dt) for wd, dt in zip(widths, dtypes)]
    out_specs = [pl.BlockSpec((tm, wd), lambda i: (i, 0)) for wd in widths]
    if do_ln:
        out_shape.append(jax.ShapeDtypeStruct((T, D), F32))
        out_specs.append(pl.BlockSpec((tm, D), lambda i: (i, 0)))
    row = lambda n: pl.BlockSpec((1, n), lambda i: (0, 0))
    return pl.pallas_call(
        functools.partial(_inproj_kernel, do_ln),
        grid=(T // tm,),
        in_specs=[pl.BlockSpec((tm, D), lambda i: (i, 0)), row(D), row(D),
                  pl.BlockSpec((D, N_PROJ_PAD), lambda i: (0, 0)), row(N_PROJ_PAD)],
        out_specs=out_specs,
        out_shape=out_shape,
        compiler_params=_params("parallel"),
        name="inproj",
    )(x, ln_g, ln_b, w, bias)


def _pad_rows(a, rows):
    if a.shape[0] == rows:
        return a
    return jnp.concatenate([a, jnp.zeros((rows - a.shape[0], a.shape[1]), a.dtype)], axis=0)


def _hgrn_kernel(C, Cb, l_real, a_ref, lb_ref, gain_ref, s0_ref, o_ref, sout_ref, st_ref):
    c = pl.program_id(1)

    @pl.when(c == 0)
    def _():
        st_ref[...] = s0_ref[...]

    W = REC_W
    a = _pad_rows(a_ref[...], C)
    q, fa, v, ga = a[:, 0:W], a[:, W:2 * W], a[:, 2 * W:3 * W], a[:, 3 * W:4 * W]
    lb = lb_ref[...]
    valid = (c * C + lax.broadcasted_iota(jnp.int32, (C, 1), 0)) < l_real
    f = lb + (1.0 - lb) * _sigmoid(fa)
    g = jnp.where(valid, jnp.log(f), 0.0)
    kk = jnp.where(valid, 1.0 - f, 0.0)
    b = _dot(_tril_ones(C), g, HIGHEST)

    bd = _block_diag_mask(W, HEAD_W)
    bd_bf = bd.astype(BF16)
    hm = _head_lane_masks(W)

    nb = C // SUB
    b3, q3, k3, v3 = (z.reshape(nb, SUB, W) for z in (b, q, kk, v))
    t_in = lax.broadcasted_iota(jnp.int32, (nb, SUB, W), 1)
    o3 = jnp.zeros((nb, SUB, W), F32)
    for s in range(SUB):
        e = jnp.exp(jnp.minimum(b3 - b3[:, s:s + 1, :], 0.0))
        p = jnp.where(t_in >= s, e * q3 * k3[:, s:s + 1, :], 0.0)
        r = _dot(p.reshape(C, W).astype(BF16), bd_bf)
        o3 = o3 + r.reshape(nb, SUB, W) * v3[:, s:s + 1, :]
    o = o3.reshape(C, W)

    t_row = lax.broadcasted_iota(jnp.int32, (C, 1), 0)
    s_col = lax.broadcasted_iota(jnp.int32, (1, C), 1)
    att = [jnp.zeros((C, C), F32) for _ in range(N_HEADS)]
    m = SUB
    while m < C:
        blk = 2 * m
        ref = jnp.broadcast_to(b.reshape(C // blk, blk, W)[:, m - 1:m, :], (C // blk, blk, W)).reshape(C, W)
        later = (t_row % blk) >= m
        qt = jnp.where(later, q * jnp.exp(jnp.minimum(b - ref, 0.0)), 0.0)
        kt = jnp.where(later, 0.0, kk * jnp.exp(jnp.minimum(ref - b, 0.0))).astype(BF16)
        same = (t_row // blk) == (s_col // blk)
        for h in range(N_HEADS):
            sc = _nt((qt * hm[h]).astype(BF16), kt)
            att[h] = att[h] + jnp.where(same, sc, 0.0)
        m = blk
    if C > SUB:
        for h in range(N_HEADS):
            o = o + _dot(att[h].astype(BF16), (v * hm[h]).astype(BF16))

    st = st_ref[...]
    o = o + _nt((q * jnp.exp(b)).astype(BF16), st.astype(BF16))
    b_end = b[C - 1:C, :]
    ke = kk * jnp.exp(b_end - b)
    st_new = st * jnp.exp(b_end) + jnp.where(bd, _tn(v.astype(BF16), ke.astype(BF16)), 0.0)
    st_ref[...] = st_new
    sout_ref[...] = st_new

    ms = _dot(o * o, bd.astype(F32), HIGHEST) * (1.0 / HEAD_W)
    y = o * lax.rsqrt(ms + NORM_EPS) * gain_ref[...] * _silu(ga)
    o_ref[...] = y[:Cb].astype(o_ref.dtype)


def _hgrn(a, lb, gain, s0, l_real):
    B, L, _ = a.shape
    Cb = min(REC_CHUNK, L)
    W = REC_W
    return pl.pallas_call(
        functools.partial(_hgrn_kernel, REC_CHUNK, Cb, l_real),
        grid=(B, L // Cb),
        in_specs=[pl.BlockSpec((None, Cb, 4 * W), lambda b, c: (b, c, 0)),
                  pl.BlockSpec((1, W), lambda b, c: (0, 0)),
                  pl.BlockSpec((1, W), lambda b, c: (0, 0)),
                  pl.BlockSpec((None, W, W), lambda b, c: (b, 0, 0))],
        out_specs=[pl.BlockSpec((None, Cb, W), lambda b, c: (b, c, 0)),
                   pl.BlockSpec((None, W, W), lambda b, c: (b, 0, 0))],
        out_shape=[jax.ShapeDtypeStruct((B, L, W), BF16), jax.ShapeDtypeStruct((B, W, W), F32)],
        scratch_shapes=[pltpu.VMEM((W, W), F32)],
        compiler_params=_params("parallel", "arbitrary"),
        name="hgrn",
    )(a, lb, gain, s0)


def _log_sigmoid(x):
    return jnp.minimum(x, 0.0) - jnp.log1p(jnp.exp(-jnp.abs(x)))


def _mlstm_kernel(C, Cb, l_real, cg_ref, gt_ref, cw_ref, cb_ref, bf_ref, gain_ref, c0_ref, n0_ref, m0_ref,
                  tail0_ref, o_ref, cout_ref, nout_ref, mout_ref, ct_ref, n_ref, m_ref, tail_ref):
    c = pl.program_id(1)

    @pl.when(c == 0)
    def _():
        ct_ref[...] = c0_ref[...]
        n_ref[...] = n0_ref[...]
        m_ref[...] = m0_ref[...]
        tail_ref[...] = tail0_ref[...]

    W = REC_W
    cg = _pad_rows(cg_ref[...], C)
    gt = _pad_rows(gt_ref[...], C)
    valid = (c * C + lax.broadcasted_iota(jnp.int32, (C, 1), 0)) < l_real

    qk_pre = cg[:, 0:2 * W]
    xp = jnp.concatenate([tail_ref[...], qk_pre], axis=0)
    conv = cb_ref[...]
    base = SUB - (CONV_TAPS - 1)
    for j in range(CONV_TAPS):
        conv = conv + cw_ref[j:j + 1, :] * xp[base + j:base + j + C, :]
    tail_ref[...] = qk_pre[C - SUB:C, :]
    qk = _silu(conv)
    qc = qk[:, 0:W]
    kc = qk[:, W:2 * W] * (HEAD_W ** -0.5)
    vc = cg[:, 2 * W:3 * W]
    oc = cg[:, 3 * W:4 * W]

    lf = jnp.where(valid, _log_sigmoid(gt + bf_ref[...]), 0.0)
    li = jnp.where(valid, gt, NEG)
    fcum = _dot(_tril_ones(C), lf, HIGHEST)
    lane_r = lax.broadcasted_iota(jnp.int32, (ROUTE_LANES, W), 0)
    lane_c = lax.broadcasted_iota(jnp.int32, (ROUTE_LANES, W), 1) // HEAD_W
    li_b = _dot(li, (lane_r == lane_c).astype(F32), HIGHEST)
    f_b = _dot(fcum, (lane_r == lane_c + N_HEADS).astype(F32), HIGHEST)
    f_t = fcum.T
    li_t = li.T

    hm = _head_lane_masks(W)
    bd = _block_diag_mask(W, HEAD_W)
    causal = lax.broadcasted_iota(jnp.int32, (C, C), 0) >= lax.broadcasted_iota(jnp.int32, (C, C), 1)
    m_prev = m_ref[...]
    kc_bf = kc.astype(BF16)
    num = jnp.zeros((C, W), F32)
    den = jnp.zeros((C, W), F32)
    cs_b = jnp.zeros((C, W), F32)
    mt_b = jnp.zeros((C, W), F32)
    for h in range(N_HEADS):
        f_col = fcum[:, N_HEADS + h:N_HEADS + h + 1]
        log_d = jnp.where(causal, f_col - f_t[N_HEADS + h:N_HEADS + h + 1, :] + li_t[h:h + 1, :], NEG)
        log_s = f_col + m_prev[:, h * HEAD_W:h * HEAD_W + 1]
        m_t = jnp.maximum(jnp.max(log_d, axis=1, keepdims=True), log_s)
        w = _nt((qc * hm[h]).astype(BF16), kc_bf) * jnp.exp(log_d - m_t)
        num = num + _dot(w.astype(BF16), (vc * hm[h]).astype(BF16))
        den = den + jnp.sum(w, axis=1, keepdims=True) * hm[h]
        cs_b = cs_b + jnp.exp(log_s - m_t) * hm[h]
        mt_b = mt_b + m_t * hm[h]

    ct = ct_ref[...]
    n_row = n_ref[...]
    num = num + cs_b * _nt(qc.astype(BF16), ct.astype(BF16))
    den = den + cs_b * _dot((qc * n_row).astype(BF16), bd.astype(BF16))
    hh = num / jnp.maximum(jnp.abs(den), jnp.exp(-mt_b))

    m_new = mt_b[C - 1:C, :]
    f_end = f_b[C - 1:C, :]
    ws = jnp.exp(f_end - f_b + li_b - m_new)
    decay = jnp.exp(f_end + m_prev - m_new)
    ct_new = ct * decay + jnp.where(bd, _tn((vc * ws).astype(BF16), kc_bf), 0.0)
    n_new = decay * n_row + jnp.sum(ws * kc, axis=0, keepdims=True)
    ct_ref[...] = ct_new
    n_ref[...] = n_new
    m_ref[...] = m_new
    cout_ref[...] = ct_new
    nout_ref[...] = n_new
    mout_ref[...] = m_new

    ms = _dot(hh * hh, bd.astype(F32), HIGHEST) * (1.0 / HEAD_W)
    y = _sigmoid(oc) * (hh * lax.rsqrt(ms + NORM_EPS) * gain_ref[...])
    o_ref[...] = y[:Cb].astype(o_ref.dtype)


def _mlstm(cg, gates, conv_w, conv_b, bias_f, gain, c0, n0, m0, tail0, l_real):
    B, L, _ = cg.shape
    Cb = min(REC_CHUNK, L)
    W = REC_W
    const = lambda shape: pl.BlockSpec(shape, lambda b, c: (0,) * len(shape))
    per_b = lambda shape: pl.BlockSpec((None,) + shape, lambda b, c: (b,) + (0,) * len(shape))
    return pl.pallas_call(
        functools.partial(_mlstm_kernel, REC_CHUNK, Cb, l_real),
        grid=(B, L // Cb),
        in_specs=[pl.BlockSpec((None, Cb, 4 * W), lambda b, c: (b, c, 0)),
                  pl.BlockSpec((None, Cb, ROUTE_LANES), lambda b, c: (b, c, 0)),
                  const((CONV_TAPS, 2 * W)), const((1, 2 * W)), const((1, ROUTE_LANES)), const((1, W)),
                  per_b((W, W)), per_b((1, W)), per_b((1, W)), per_b((SUB, 2 * W))],
        out_specs=[pl.BlockSpec((None, Cb, W), lambda b, c: (b, c, 0)),
                   per_b((W, W)), per_b((1, W)), per_b((1, W))],
        out_shape=[jax.ShapeDtypeStruct((B, L, W), BF16), jax.ShapeDtypeStruct((B, W, W), F32),
                   jax.ShapeDtypeStruct((B, 1, W), F32), jax.ShapeDtypeStruct((B, 1, W), F32)],
        scratch_shapes=[pltpu.VMEM((W, W), F32), pltpu.VMEM((1, W), F32), pltpu.VMEM((1, W), F32),
                        pltpu.VMEM((SUB, 2 * W), F32)],
        compiler_params=_params("parallel", "arbitrary"),
        name="mlstm",
    )(cg, gates, conv_w, conv_b, bias_f, gain, c0, n0, m0, tail0)


def _diff_finish(o1, o2, lam, gain, out_scale):
    o = o1 - lam * o2
    ms = jnp.mean(o * o, axis=-1, keepdims=True)
    return out_scale * (o * lax.rsqrt(ms + NORM_EPS) * gain)


def _attn_kernel(T, out_scale, qi_ref, ki_ref, q_ref, k_ref, v_ref, lam_ref, gain_ref, o_ref,
                 qq_sc, m_sc, l_sc, acc_sc):
    step_id = pl.program_id(1)
    qi = qi_ref[step_id]
    ki = ki_ref[step_id]
    heads = [slice(h * ATT_W, (h + 1) * ATT_W) for h in range(N_HEADS)]

    @pl.when(ki == 0)
    def _():
        lane = lax.broadcasted_iota(jnp.int32, (1, ATT_W), 1)
        for h in range(N_HEADS):
            q = q_ref[:, heads[h]]
            zero = jnp.zeros_like(q)
            qq_sc[h, 0:T, :] = jnp.where(lane < HEAD_W, q, zero)
            qq_sc[h, T:2 * T, :] = jnp.where(lane >= HEAD_W, q, zero)
        m_sc[...] = jnp.full(m_sc.shape, NEG, F32)
        l_sc[...] = jnp.zeros(l_sc.shape, F32)
        acc_sc[...] = jnp.zeros(acc_sc.shape, F32)

    def step(diagonal):
        for h in range(N_HEADS):
            s = _nt(qq_sc[h], k_ref[:, heads[h]])
            if diagonal:
                row = lax.broadcasted_iota(jnp.int32, (2 * T, 1), 0) % T
                col = lax.broadcasted_iota(jnp.int32, (1, T), 1)
                s = jnp.where(row >= col, s, NEG)
            m_old = m_sc[h]
            m_new = jnp.maximum(m_old, jnp.max(s, axis=1, keepdims=True))
            alpha = jnp.exp(m_old - m_new)
            p = jnp.exp(s - m_new)
            l_sc[h] = alpha * l_sc[h] + jnp.sum(p, axis=1, keepdims=True)
            acc_sc[h] = alpha * acc_sc[h] + _dot(p.astype(BF16), v_ref[:, heads[h]])
            m_sc[h] = m_new

    @pl.when(ki < qi)
    def _():
        step(False)

    @pl.when(ki == qi)
    def _():
        step(True)
        for h in range(N_HEADS):
            on = acc_sc[h] / l_sc[h]
            y = _diff_finish(on[:T], on[T:], lam_ref[...], gain_ref[:, heads[h]], out_scale)
            o_ref[:, heads[h]] = y.astype(o_ref.dtype)


def _attn_prompt(q, k, v, lam_row, gain, out_scale):
    B, L, _ = q.shape
    T = ATT_TILE if L % ATT_TILE == 0 else SEQ_TILE
    n = L // T
    pairs = [(a, b) for a in range(n) for b in range(a + 1)]
    qi_arr = jnp.asarray([a for a, _ in pairs], jnp.int32)
    ki_arr = jnp.asarray([b for _, b in pairs], jnp.int32)
    q_spec = pl.BlockSpec((None, T, ATT_ALL), lambda b, p, qi, ki: (b, qi[p], 0))
    kv_spec = pl.BlockSpec((None, T, ATT_ALL), lambda b, p, qi, ki: (b, ki[p], 0))
    grid_spec = pltpu.PrefetchScalarGridSpec(
        num_scalar_prefetch=2,
        grid=(B, len(pairs)),
        in_specs=[q_spec, kv_spec, kv_spec,
                  pl.BlockSpec((1, ATT_W), lambda b, p, qi, ki: (0, 0)),
                  pl.BlockSpec((1, ATT_ALL), lambda b, p, qi, ki: (0, 0))],
        out_specs=q_spec,
        scratch_shapes=[pltpu.VMEM((N_HEADS, 2 * T, ATT_W), BF16), pltpu.VMEM((N_HEADS, 2 * T, 1), F32),
                        pltpu.VMEM((N_HEADS, 2 * T, 1), F32), pltpu.VMEM((N_HEADS, 2 * T, ATT_W), F32)],
    )
    return pl.pallas_call(
        functools.partial(_attn_kernel, T, out_scale),
        grid_spec=grid_spec,
        out_shape=jax.ShapeDtypeStruct((B, L, ATT_ALL), BF16),
        compiler_params=_params("parallel", "arbitrary"),
        name="attn_prompt",
    )(qi_arr, ki_arr, q, k, v, lam_row, gain)


def _dec_attn_kernel(npg, ls, page, out_scale, pt_ref, q_ref, kn_ref, vn_ref, lam_ref, gain_ref, *refs):
    del pt_ref
    k_refs, v_refs = refs[:npg], refs[npg:2 * npg]
    o_ref, qs_sc, m_sc, l_sc, acc_sc = refs[2 * npg:]
    j = pl.program_id(1)
    rows = 2 * N_HEADS * ls
    cols = page * N_HEADS
    row_head = lax.broadcasted_iota(jnp.int32, (rows, 1), 0) // (2 * ls)

    @pl.when(j == 0)
    def _():
        q = q_ref[...].astype(F32)
        lane = lax.broadcasted_iota(jnp.int32, (1, ATT_W), 1)
        parts = []
        for h in range(N_HEADS):
            qh = q[:, h * ATT_W:(h + 1) * ATT_W]
            parts += [jnp.where(lane < HEAD_W, qh, 0.0), jnp.where(lane >= HEAD_W, qh, 0.0)]
        qs_sc[...] = jnp.concatenate(parts, axis=0).astype(BF16)
        m_sc[...] = jnp.full(m_sc.shape, NEG, F32)
        l_sc[...] = jnp.zeros(l_sc.shape, F32)
        acc_sc[...] = jnp.zeros(acc_sc.shape, F32)

    qs = qs_sc[...]

    def update(scores, values):
        m_old = m_sc[...]
        m_new = m_old
        for s in scores:
            m_new = jnp.maximum(m_new, jnp.max(s, axis=1, keepdims=True))
        alpha = jnp.exp(m_old - m_new)
        l_new = alpha * l_sc[...]
        acc = alpha * acc_sc[...]
        for s, val in zip(scores, values):
            p = jnp.exp(s - m_new)
            l_new = l_new + jnp.sum(p, axis=1, keepdims=True)
            acc = acc + _dot(p.astype(BF16), val)
        l_sc[...] = l_new
        acc_sc[...] = acc
        m_sc[...] = m_new

    own_head = (lax.broadcasted_iota(jnp.int32, (1, cols), 1) % N_HEADS) == row_head
    update([jnp.where(own_head, _nt(qs, kr[...].astype(BF16)), NEG) for kr in k_refs],
           [vr[...].astype(BF16) for vr in v_refs])

    @pl.when(j == pl.num_programs(1) - 1)
    def _():
        n_new = ls * N_HEADS
        kn = _pad_rows(kn_ref[...], ATT_W).astype(BF16)
        vn = _pad_rows(vn_ref[...], ATT_W).astype(BF16)
        col = lax.broadcasted_iota(jnp.int32, (1, ATT_W), 1)
        q_idx = lax.broadcasted_iota(jnp.int32, (rows, 1), 0) % ls
        ok = (col % N_HEADS == row_head) & (col // N_HEADS <= q_idx) & (col < n_new)
        update([jnp.where(ok, _nt(qs, kn), NEG)], [vn])
        on = acc_sc[...] / l_sc[...]
        gain = gain_ref[...]
        parts = []
        for h in range(N_HEADS):
            r0 = h * 2 * ls
            parts.append(_diff_finish(on[r0:r0 + ls], on[r0 + ls:r0 + 2 * ls], lam_ref[...],
                                      gain[:, h * ATT_W:(h + 1) * ATT_W], out_scale))
        o_ref[...] = jnp.concatenate(parts, axis=1).astype(o_ref.dtype)


def _attn_sample(q, k_new, v_new, cache_k, cache_v, page_table, layer, lam_row, gain, out_scale):
    B, ls, _ = q.shape
    n_pages = page_table.shape[1]
    page = cache_k.shape[2]
    cols = page * N_HEADS
    npg = PAGES_PER_STEP if n_pages % PAGES_PER_STEP == 0 else 1
    ck = cache_k.reshape(cache_k.shape[0], cache_k.shape[1], cols, ATT_W)
    cv = cache_v.reshape(cache_v.shape[0], cache_v.shape[1], cols, ATT_W)
    kn = k_new.reshape(B, ls * N_HEADS, ATT_W)
    vn = v_new.reshape(B, ls * N_HEADS, ATT_W)

    def page_spec(i):
        return pl.BlockSpec((None, None, cols, ATT_W), lambda b, j, pt: (layer, pt[b, j * npg + i], 0, 0))

    per_b = pl.BlockSpec((None, ls, ATT_ALL), lambda b, j, pt: (b, 0, 0))
    new_spec = pl.BlockSpec((None, ls * N_HEADS, ATT_W), lambda b, j, pt: (b, 0, 0))
    rows = 2 * N_HEADS * ls
    grid_spec = pltpu.PrefetchScalarGridSpec(
        num_scalar_prefetch=1,
        grid=(B, n_pages // npg),
        in_specs=[per_b, new_spec, new_spec,
                  pl.BlockSpec((1, ATT_W), lambda b, j, pt: (0, 0)),
                  pl.BlockSpec((1, ATT_ALL), lambda b, j, pt: (0, 0))]
                 + [page_spec(i) for i in range(npg)] * 2,
        out_specs=per_b,
        scratch_shapes=[pltpu.VMEM((rows, ATT_W), BF16), pltpu.VMEM((rows, 1), F32), pltpu.VMEM((rows, 1), F32),
                        pltpu.VMEM((rows, ATT_W), F32)],
    )
    return pl.pallas_call(
        functools.partial(_dec_attn_kernel, npg, ls, page, out_scale),
        grid_spec=grid_spec,
        out_shape=jax.ShapeDtypeStruct((B, ls, ATT_ALL), BF16),
        compiler_params=_params("parallel", "arbitrary"),
        name="attn_sample",
    )(page_table, q, kn, vn, lam_row, gain, *([ck] * npg), *([cv] * npg))


def _outproj_kernel(alpha, ma_ref, mb_ref, mc_ref, h_ref, wo_ref, g_ref, b_ref, wr_ref, br_ref,
                    h1_ref, eid_ref, gate_ref):
    W = REC_W
    mix = (_dot(ma_ref[...], wo_ref[0:W, :]) + _dot(mb_ref[...], wo_ref[W:W + ATT_ALL, :])
           + _dot(mc_ref[...], wo_ref[W + ATT_ALL:2 * W + ATT_ALL, :]))
    h1 = _layernorm(alpha * h_ref[...] + mix, g_ref[...], b_ref[...])
    h1_ref[...] = h1

    logits = _dot(h1.astype(BF16), wr_ref[...]) + br_ref[...]
    lane = lax.broadcasted_iota(jnp.int32, (1, ROUTE_LANES), 1)
    lane_f = lane.astype(F32)
    far = float(ROUTE_LANES)

    gl = jnp.where(lane < N_GROUPS, logits, NEG)
    g_max = jnp.max(gl, axis=1, keepdims=True)
    g_top = 1.0 / jnp.sum(jnp.exp(gl - g_max), axis=1, keepdims=True)
    g_idx = jnp.min(jnp.where(gl == g_max, lane_f, far), axis=1, keepdims=True)

    e_lane = lane_f - float(EXPERT_LANE0)
    in_group = (e_lane >= g_idx * GROUP_EXPERTS) & (e_lane < (g_idx + 1.0) * GROUP_EXPERTS)
    el = jnp.where(in_group, logits, NEG)
    e_max = jnp.max(el, axis=1, keepdims=True)
    e_sum = jnp.sum(jnp.exp(el - e_max), axis=1, keepdims=True)
    idx1 = jnp.min(jnp.where(el == e_max, lane_f, far), axis=1, keepdims=True)
    el2 = jnp.where(lane_f == idx1, NEG, el)
    e_max2 = jnp.max(el2, axis=1, keepdims=True)
    idx2 = jnp.min(jnp.where(el2 == e_max2, lane_f, far), axis=1, keepdims=True)
    p1 = 1.0 / e_sum
    p2 = jnp.exp(e_max2 - e_max) / e_sum
    tot = p1 + p2
    gate_ref[...] = jnp.where(lane == 0, g_top * (p1 / tot), jnp.where(lane == 1, g_top * (p2 / tot), 0.0))
    eid = jnp.where(lane == 0, idx1 - float(EXPERT_LANE0), jnp.where(lane == 1, idx2 - float(EXPERT_LANE0), 0.0))
    eid_ref[...] = eid.astype(jnp.int32)


def _outproj(ma, mb, mc, h, wo, g, b, wr, br, alpha):
    T, D = h.shape
    tm = min(ROW_TILE, T)
    tile = lambda n: pl.BlockSpec((tm, n), lambda i: (i, 0))
    const = lambda r, n: pl.BlockSpec((r, n), lambda i: (0, 0))
    return pl.pallas_call(
        functools.partial(_outproj_kernel, alpha),
        grid=(T // tm,),
        in_specs=[tile(REC_W), tile(ATT_ALL), tile(REC_W), tile(D), const(2 * REC_W + ATT_ALL, D),
                  const(1, D), const(1, D), const(D, ROUTE_LANES), const(1, ROUTE_LANES)],
        out_specs=[tile(D), tile(ROUTE_LANES), tile(ROUTE_LANES)],
        out_shape=[jax.ShapeDtypeStruct((T, D), F32), jax.ShapeDtypeStruct((T, ROUTE_LANES), jnp.int32),
                   jax.ShapeDtypeStruct((T, ROUTE_LANES), F32)],
        compiler_params=_params("parallel"),
        name="outproj",
    )(ma, mb, mc, h, wo, g, b, wr, br)


def _expert_kernel(row_ref, blke_ref, nval_ref, x_hbm, wg_ref, wu_ref, wd_ref, out_hbm,
                   xbuf, ybuf, gsem, ssem):
    del blke_ref
    i = pl.program_id(0)
    nblk = pl.num_programs(0)
    cur = i % 2

    def start_gather(blk, buf):
        @pl.when(nval_ref[blk] > 0)
        def _():
            def body(r, carry):
                tok = lax.shift_right_logical(row_ref[blk, r], 1)
                pltpu.make_async_copy(x_hbm.at[pl.ds(tok, 1), :], xbuf.at[buf, pl.ds(r, 1), :], gsem.at[buf]).start()
                return carry
            lax.fori_loop(0, MOE_ROWS, body, 0, unroll=SUB)

    def start_scatter(blk, buf):
        def one(r):
            code = row_ref[blk, r]
            pltpu.make_async_copy(ybuf.at[buf, pl.ds(r, 1), :],
                                  out_hbm.at[code & 1, pl.ds(lax.shift_right_logical(code, 1), 1), :],
                                  ssem.at[buf]).start()

        n = nval_ref[blk]
        groups = lax.shift_right_logical(n, 3)

        def group(g, carry):
            for u in range(SUB):
                one(g * SUB + u)
            return carry
        lax.fori_loop(0, groups, group, 0)

        def rest(r, carry):
            one(r)
            return carry
        lax.fori_loop(groups * SUB, n, rest, 0)

    def wait_scatter(blk, buf):
        n = nval_ref[blk]
        n_tiled = pl.multiple_of(lax.shift_right_logical(n, 3) * SUB, SUB)

        @pl.when(n_tiled > 0)
        def _():
            pltpu.make_async_copy(ybuf.at[buf, pl.ds(0, n_tiled), :], out_hbm.at[0, pl.ds(0, n_tiled), :],
                                  ssem.at[buf]).wait()

        def rest(r, carry):
            pltpu.make_async_copy(ybuf.at[buf, pl.ds(r, 1), :], out_hbm.at[0, pl.ds(0, 1), :], ssem.at[buf]).wait()
            return carry
        lax.fori_loop(n_tiled, n, rest, 0)

    @pl.when(i == 0)
    def _():
        start_gather(0, 0)

    @pl.when(i + 1 < nblk)
    def _():
        start_gather(i + 1, 1 - cur)

    @pl.when(i >= 2)
    def _():
        wait_scatter(i - 2, cur)

    @pl.when(nval_ref[i] > 0)
    def _():
        pltpu.make_async_copy(x_hbm.at[pl.ds(0, MOE_ROWS), :], xbuf.at[cur], gsem.at[cur]).wait()
        xb = xbuf[cur].astype(BF16)
        hid = _silu(_dot(xb, wg_ref[...])) * _dot(xb, wu_ref[...])
        ybuf[cur] = _dot(hid.astype(BF16), wd_ref[...])
        start_scatter(i, cur)

    @pl.when(i == nblk - 1)
    def _():
        @pl.when(i >= 1)
        def _():
            wait_scatter(i - 1, 1 - cur)
        wait_scatter(i, cur)


def _experts(x, row_code, blk_e, blk_nval, wg, wu, wd):
    T, D = x.shape
    n_blocks = blk_e.shape[0]
    de = wg.shape[2]
    grid_spec = pltpu.PrefetchScalarGridSpec(
        num_scalar_prefetch=3,
        grid=(n_blocks,),
        in_specs=[pl.BlockSpec(memory_space=pl.ANY),
                  pl.BlockSpec((None, D, de), lambda i, code, be, nv: (be[i], 0, 0)),
                  pl.BlockSpec((None, D, de), lambda i, code, be, nv: (be[i], 0, 0)),
                  pl.BlockSpec((None, de, D), lambda i, code, be, nv: (be[i], 0, 0))],
        out_specs=pl.BlockSpec(memory_space=pl.ANY),
        scratch_shapes=[pltpu.VMEM((2, MOE_ROWS, D), F32), pltpu.VMEM((2, MOE_ROWS, D), F32),
                        pltpu.SemaphoreType.DMA((2,)), pltpu.SemaphoreType.DMA((2,))],
    )
    return pl.pallas_call(
        _expert_kernel,
        grid_spec=grid_spec,
        out_shape=jax.ShapeDtypeStruct((2, T, D), F32),
        compiler_params=_params("arbitrary"),
        name="experts",
    )(row_code, blk_e, blk_nval, x, wg, wu, wd)


def _dispatch(eid, T):
    K = 2
    A = T * K
    flat_e = eid[:, :K].reshape(-1)
    onehot = (flat_e[:, None] == jnp.arange(N_EXPERTS, dtype=jnp.int32)[None, :]).astype(F32)
    a_pad = -(-A // MOE_ROWS) * MOE_ROWS
    oh3 = jnp.pad(onehot, ((0, a_pad - A), (0, 0))).reshape(a_pad // MOE_ROWS, MOE_ROWS, N_EXPERTS)
    tril = jnp.tril(jnp.ones((MOE_ROWS, MOE_ROWS), F32))
    within = jnp.einsum('ts,bse->bte', tril, oh3, precision=HIGHEST)
    blk_tot = within[:, -1, :]
    before = jnp.cumsum(blk_tot, axis=0) - blk_tot
    rank = jnp.sum((within + before[:, None, :]) * oh3, axis=-1).reshape(a_pad)[:A] - 1.0
    counts = jnp.sum(blk_tot, axis=0).astype(jnp.int32)
    padded = (counts + MOE_ROWS - 1) // MOE_ROWS * MOE_ROWS
    pad_end = jnp.cumsum(padded)
    pad_start = pad_end - padded
    dest = jnp.sum(onehot * pad_start.astype(F32)[None, :], axis=-1) + rank
    n_blocks = -(-A // MOE_ROWS) + N_EXPERTS
    R = n_blocks * MOE_ROWS
    row_code = jnp.zeros((R,), jnp.int32).at[dest.astype(jnp.int32)].set(jnp.arange(A, dtype=jnp.int32))
    blk_start = jnp.arange(n_blocks, dtype=jnp.int32) * MOE_ROWS
    blk_e = jnp.minimum(jnp.sum((pad_end[None, :] <= blk_start[:, None]).astype(jnp.int32), axis=1), N_EXPERTS - 1)
    blk_nval = jnp.clip((pad_start + counts)[blk_e] - blk_start, 0, MOE_ROWS).astype(jnp.int32)
    return row_code.reshape(n_blocks, MOE_ROWS), blk_e.astype(jnp.int32), blk_nval


def _combine_kernel(alpha, h_ref, y_ref, gate_ref, g_ref, b_ref, o_ref):
    gate = gate_ref[...]
    y = gate[:, 0:1] * y_ref[0] + gate[:, 1:2] * y_ref[1]
    o_ref[...] = _layernorm(alpha * h_ref[...] + y, g_ref[...], b_ref[...])


def _combine(h1, slots, gates, g, b, alpha):
    T, D = h1.shape
    tm = min(ROW_TILE, T)
    return pl.pallas_call(
        functools.partial(_combine_kernel, alpha),
        grid=(T // tm,),
        in_specs=[pl.BlockSpec((tm, D), lambda i: (i, 0)), pl.BlockSpec((2, tm, D), lambda i: (0, i, 0)),
                  pl.BlockSpec((tm, ROUTE_LANES), lambda i: (i, 0)),
                  pl.BlockSpec((1, D), lambda i: (0, 0)), pl.BlockSpec((1, D), lambda i: (0, 0))],
        out_specs=pl.BlockSpec((tm, D), lambda i: (i, 0)),
        out_shape=jax.ShapeDtypeStruct((T, D), F32),
        compiler_params=_params("parallel"),
        name="combine",
    )(h1, slots, gates, g, b)


def _block_diag(blocks):
    B, H, n, _ = blocks.shape
    eye = jnp.eye(H, dtype=blocks.dtype)
    return jnp.einsum('bhij,hg->bhigj', blocks, eye).reshape(B, H * n, H * n)


def _diag_blocks(mat, H):
    B, n = mat.shape[0], mat.shape[1] // H
    m5 = mat.reshape(B, H, n, H, n)
    return jnp.stack([m5[:, h, :, h, :] for h in range(H)], axis=1)


def _layer(x, do_ln, B, L, l_real, lw, state, past):
    outs = _inproj(x, lw['ln_g'], lw['ln_b'], lw['w_in'], lw['b_in'], do_ln)
    a, qb, kb, vb, cg, gates, kb16, vb16 = outs[:8]
    h = outs[8] if do_ln else x
    s3 = lambda z: z.reshape(B, L, z.shape[-1])

    mix_a, st_a = _hgrn(s3(a), lw['lb'], lw['norm_a'], state['hgrn'], l_real)
    mix_c, ct, n_c, m_c = _mlstm(s3(cg), s3(gates), lw['conv_w'], lw['conv_b'], lw['b_f'], lw['norm_c'],
                                 state['c'], state['n'], state['m'], state['tail'], l_real)
    if past is None:
        mix_b = _attn_prompt(s3(qb), s3(kb16), s3(vb16), lw['lam'], lw['norm_b'], lw['out_scale'])
    else:
        mix_b = _attn_sample(s3(qb), s3(kb), s3(vb), past[0], past[1], past[2], lw['layer'], lw['lam'],
                             lw['norm_b'], lw['out_scale'])

    flat = lambda z: z.reshape(B * L, z.shape[-1])
    h1, eid, gate = _outproj(flat(mix_a), flat(mix_b), flat(mix_c), h, lw['w_out'], lw['ln1_g'], lw['ln1_b'],
                             lw['w_route'], lw['b_route'], lw['alpha'])
    row_code, blk_e, blk_nval = _dispatch(eid, B * L)
    slots = _experts(h1, row_code, blk_e, blk_nval, lw['w_eg'], lw['w_eu'], lw['w_ed'])
    h2 = _combine(h1, slots, gate, lw['ln2_g'], lw['ln2_b'], lw['alpha'])

    new_state = {
        'k': s3(kb), 'v': s3(vb),
        'hgrn': jnp.swapaxes(_diag_blocks(st_a, N_HEADS), -1, -2),
        'c': _diag_blocks(ct, N_HEADS),
        'n': n_c.reshape(B, N_HEADS, HEAD_W),
        'm': m_c.reshape(B, N_HEADS, HEAD_W)[:, :, 0],
        'conv': s3(cg)[:, l_real - (CONV_TAPS - 1):l_real, :2 * REC_W],
    }
    return h2, new_state


def kernel(x_prompt, x_sample, cache_k, cache_v, page_table, state_hgrn, state_mlstm_c, state_mlstm_n, state_mlstm_m, state_mlstm_conv, meta_tokens, ln_emb_g, ln_emb_b, w_in, b_in, b_mlstm_f, conv_w, conv_b, hgrn_lb_logits, lambda_q1, lambda_k1, lambda_q2, lambda_k2, norm_a, norm_b, norm_c, w_out, ln1_g, ln1_b, w_router_group, b_router_group, w_router_expert, b_router_expert, w_exp_gate, w_exp_up, w_exp_down, ln2_g, ln2_b):
    depth = w_in.shape[0]
    Bp, Lq, D = x_prompt.shape
    Bs, Ls, _ = x_sample.shape
    n_meta = meta_tokens.shape[0]
    l_real = n_meta + Lq
    Lp = -(-l_real // SEQ_TILE) * SEQ_TILE
    alpha = (2.0 * depth) ** 0.25
    row = lambda z: z.reshape(1, -1).astype(F32)

    meta = jnp.broadcast_to(meta_tokens[None].astype(x_prompt.dtype), (Bp, n_meta, D))
    xp = jnp.concatenate([meta, x_prompt, jnp.zeros((Bp, Lp - l_real, D), x_prompt.dtype)], axis=1)
    hp = xp.reshape(Bp * Lp, D)
    hs = x_sample.reshape(Bs * Ls, D)

    lb_sm = jax.nn.softmax(hgrn_lb_logits.astype(F32), axis=0)
    lb_all = jnp.cumsum(lb_sm, axis=0) - lb_sm[0]

    zeros_p = {
        'hgrn': jnp.zeros((Bp, REC_W, REC_W), F32), 'c': jnp.zeros((Bp, REC_W, REC_W), F32),
        'n': jnp.zeros((Bp, 1, REC_W), F32), 'm': jnp.zeros((Bp, 1, REC_W), F32),
        'tail': jnp.zeros((Bp, SUB, 2 * REC_W), F32),
    }

    res_p, res_s = [], []
    for l in range(depth):
        lam_init = 0.8 - 0.6 * math.exp(-0.3 * l)
        lam = (jnp.exp(jnp.sum(lambda_q1[l].astype(F32) * lambda_k1[l].astype(F32)))
               - jnp.exp(jnp.sum(lambda_q2[l].astype(F32) * lambda_k2[l].astype(F32))) + lam_init)
        pad_cols = N_PROJ_PAD - N_PROJ_RAW
        bias_f = jnp.zeros((ROUTE_LANES,), F32).at[N_HEADS:2 * N_HEADS].set(b_mlstm_f[l].astype(F32))
        w_route = jnp.concatenate([w_router_group[l], w_router_expert[l]], axis=1)
        b_route = jnp.concatenate([b_router_group[l], b_router_expert[l]])
        r_pad = ROUTE_LANES - w_route.shape[1]
        lw = {
            'layer': l, 'alpha': alpha, 'out_scale': 1.0 - lam_init,
            'ln_g': row(ln_emb_g), 'ln_b': row(ln_emb_b),
            'w_in': jnp.pad(w_in[l], ((0, 0), (0, pad_cols))).astype(BF16),
            'b_in': row(jnp.pad(b_in[l], (0, pad_cols))),
            'lb': row(lb_all[l]), 'norm_a': row(norm_a[l]), 'norm_b': row(norm_b[l]), 'norm_c': row(norm_c[l]),
            'lam': jnp.full((1, ATT_W), lam, F32),
            'conv_w': conv_w[l].astype(F32), 'conv_b': row(conv_b[l]), 'b_f': row(bias_f),
            'w_out': w_out[l].astype(BF16), 'ln1_g': row(ln1_g[l]), 'ln1_b': row(ln1_b[l]),
            'w_route': jnp.pad(w_route, ((0, 0), (0, r_pad))).astype(BF16), 'b_route': row(jnp.pad(b_route, (0, r_pad))),
            'w_eg': w_exp_gate[l].astype(BF16), 'w_eu': w_exp_up[l].astype(BF16), 'w_ed': w_exp_down[l].astype(BF16),
            'ln2_g': row(ln2_g[l]), 'ln2_b': row(ln2_b[l]),
        }
        hp, st_p = _layer(hp, l == 0, Bp, Lp, l_real, lw, zeros_p, None)

        state_s = {
            'hgrn': _block_diag(jnp.swapaxes(state_hgrn[l].astype(F32), -1, -2)),
            'c': _block_diag(state_mlstm_c[l].astype(F32)),
            'n': state_mlstm_n[l].astype(F32).reshape(Bs, 1, REC_W),
            'm': jnp.repeat(state_mlstm_m[l].astype(F32), HEAD_W, axis=-1).reshape(Bs, 1, REC_W),
            'tail': jnp.pad(state_mlstm_conv[l].astype(F32), ((0, 0), (SUB - (CONV_TAPS - 1), 0), (0, 0))),
        }
        hs, st_s = _layer(hs, l == 0, Bs, Ls, Ls, lw, state_s, (cache_k, cache_v, page_table))
        res_p.append(st_p)
        res_s.append(st_s)

    y_prompt = hp.reshape(Bp, Lp, D)[:, n_meta:l_real]
    y_sample = hs.reshape(Bs, Ls, D)
    stack = lambda res, key, f=lambda z: z: jnp.stack([f(r[key]) for r in res])
    kv_p = lambda z: z[:, :l_real].reshape(Bp, l_real, N_HEADS, ATT_W)
    kv_s = lambda z: z.reshape(Bs, Ls, N_HEADS, ATT_W)
    return (y_prompt, y_sample,
            stack(res_p, 'k', kv_p), stack(res_p, 'v', kv_p), stack(res_s, 'k', kv_s), stack(res_s, 'v', kv_s),
            stack(res_p, 'hgrn'), stack(res_s, 'hgrn'),
            stack(res_p, 'c'), stack(res_s, 'c'),
            stack(res_p, 'n'), stack(res_s, 'n'),
            stack(res_p, 'm'), stack(res_s, 'm'),
            stack(res_p, 'conv'), stack(res_s, 'conv'))
```

```python
import functools
import math

import jax
import jax.numpy as jnp
from jax import lax
from jax.experimental import pallas as pl
from jax.experimental.pallas import tpu as pltpu

F32 = jnp.float32
BF16 = jnp.bfloat16
HIGHEST = lax.Precision.HIGHEST

N_META = 16
N_HEADS = 4
HEAD_W = 64
ATT_W = 2 * HEAD_W
REC_W = N_HEADS * HEAD_W
ATT_ALL = N_HEADS * ATT_W
CONV_TAPS = 4
N_GROUPS = 4
GROUP_EXPERTS = 8
N_EXPERTS = N_GROUPS * GROUP_EXPERTS
ROUTE_LANES = 128
EXPERT_LANE0 = N_GROUPS
MOE_ROWS = 128
LN_EPS = 1e-5
NORM_EPS = 1e-6
NEG = -1e30

SEQ_TILE = 128
REC_CHUNK = 128
SUB = 8
ATT_TILE = 384
ATT_ROWS = 768
PAGES_PER_STEP = 8
ROW_TILE = 256
VMEM_LIMIT = 48 * 1024 * 1024

IN_WIDTHS = (4 * REC_W, ATT_ALL, ATT_ALL, ATT_ALL, 4 * REC_W, ROUTE_LANES)
N_PROJ_RAW = 4 * REC_W + 3 * ATT_ALL + 4 * REC_W + 2 * N_HEADS
N_PROJ_PAD = sum(IN_WIDTHS)


def _nt(a, b):
    return lax.dot_general(a, b, (((1,), (1,)), ((), ())), preferred_element_type=F32)


def _tn(a, b):
    return lax.dot_general(a, b, (((0,), (0,)), ((), ())), preferred_element_type=F32)


def _dot(a, b, precision=None):
    return jnp.dot(a, b, preferred_element_type=F32, precision=precision)


def _layernorm(x, g, b):
    mu = jnp.mean(x, axis=-1, keepdims=True)
    xc = x - mu
    var = jnp.mean(xc * xc, axis=-1, keepdims=True)
    return xc * lax.rsqrt(var + LN_EPS) * g + b


def _sigmoid(x):
    return 1.0 / (1.0 + jnp.exp(-x))


def _silu(x):
    return x * _sigmoid(x)


def _head_lane_masks(width):
    lane = lax.broadcasted_iota(jnp.int32, (1, width), 1)
    per = width // N_HEADS
    return [(lane // per == h).astype(F32) for h in range(N_HEADS)]


def _block_diag_mask(n, blk):
    r = lax.broadcasted_iota(jnp.int32, (n, n), 0) // blk
    c = lax.broadcasted_iota(jnp.int32, (n, n), 1) // blk
    return r == c


def _tril_ones(n):
    r = lax.broadcasted_iota(jnp.int32, (n, n), 0)
    c = lax.broadcasted_iota(jnp.int32, (n, n), 1)
    return (r >= c).astype(F32)


def _params(*sem):
    return pltpu.CompilerParams(dimension_semantics=sem, vmem_limit_bytes=VMEM_LIMIT)


def _inproj_kernel(do_ln, x_ref, g_ref, b_ref, w_ref, bias_ref, *outs):
    x = x_ref[...]
    if do_ln:
        x = _layernorm(x, g_ref[...], b_ref[...])
        outs[-1][...] = x
    xb = x.astype(BF16)
    col = 0
    for idx, width in enumerate(IN_WIDTHS):
        acc = _dot(xb, w_ref[:, col:col + width]) + bias_ref[:, col:col + width]
        if idx == 1:
            acc = acc * (HEAD_W ** -0.5)
        outs[idx][...] = acc.astype(outs[idx].dtype)
        if idx in (2, 3):
            outs[len(IN_WIDTHS) + idx - 2][...] = acc.astype(BF16)
        col += width


def _inproj(x, ln_g, ln_b, w, bias, do_ln):
    T, D = x.shape
    tm = min(ROW_TILE, T)
    widths = IN_WIDTHS + (ATT_ALL, ATT_ALL)
    dtypes = (F32, BF16, F32, F32, F32, F32, BF16, BF16)
    out_shape = [jax.ShapeDtypeStruct((T, wd), dt) for wd, dt in zip(widths, dtypes)]
    out_specs = [pl.BlockSpec((tm, wd), lambda i: (i, 0)) for wd in widths]
    if do_ln:
        out_shape.append(jax.ShapeDtypeStruct((T, D), F32))
        out_specs.append(pl.BlockSpec((tm, D), lambda i: (i, 0)))
    row = lambda n: pl.BlockSpec((1, n), lambda i: (0, 0))
    return pl.pallas_call(
        functools.partial(_inproj_kernel, do_ln),
        grid=(T // tm,),
        in_specs=[pl.BlockSpec((tm, D), lambda i: (i, 0)), row(D), row(D),
                  pl.BlockSpec((D, N_PROJ_PAD), lambda i: (0, 0)), row(N_PROJ_PAD)],
        out_specs=out_specs,
        out_shape=out_shape,
        compiler_params=_params("parallel"),
        name="inproj",
    )(x, ln_g, ln_b, w, bias)


def _pad_rows(a, rows):
    if a.shape[0] == rows:
        return a
    return jnp.concatenate([a, jnp.zeros((rows - a.shape[0], a.shape[1]), a.dtype)], axis=0)


def _hgrn_kernel(C, Cb, l_real, a_ref, lb_ref, gain_ref, s0_ref, o_ref, sout_ref, st_ref):
    c = pl.program_id(1)

    @pl.when(c == 0)
    def _():
        st_ref[...] = s0_ref[...]

    W = REC_W
    a = _pad_rows(a_ref[...], C)
    q, fa, v, ga = a[:, 0:W], a[:, W:2 * W], a[:, 2 * W:3 * W], a[:, 3 * W:4 * W]
    lb = lb_ref[...]
    valid = (c * C + lax.broadcasted_iota(jnp.int32, (C, 1), 0)) < l_real
    f = lb + (1.0 - lb) * _sigmoid(fa)
    g = jnp.where(valid, jnp.log(f), 0.0)
    kk = jnp.where(valid, 1.0 - f, 0.0)
    b = _dot(_tril_ones(C), g, HIGHEST)

    bd = _block_diag_mask(W, HEAD_W)
    bd_bf = bd.astype(BF16)
    hm = _head_lane_masks(W)

    nb = C // SUB
    b3, q3, k3, v3 = (z.reshape(nb, SUB, W) for z in (b, q, kk, v))
    t_in = lax.broadcasted_iota(jnp.int32, (nb, SUB, W), 1)
    o3 = jnp.zeros((nb, SUB, W), F32)
    for s in range(SUB):
        e = jnp.exp(jnp.minimum(b3 - b3[:, s:s + 1, :], 0.0))
        p = jnp.where(t_in >= s, e * q3 * k3[:, s:s + 1, :], 0.0)
        r = _dot(p.reshape(C, W).astype(BF16), bd_bf)
        o3 = o3 + r.reshape(nb, SUB, W) * v3[:, s:s + 1, :]
    o = o3.reshape(C, W)

    t_row = lax.broadcasted_iota(jnp.int32, (C, 1), 0)
    s_col = lax.broadcasted_iota(jnp.int32, (1, C), 1)
    att = [jnp.zeros((C, C), F32) for _ in range(N_HEADS)]
    m = SUB
    while m < C:
        blk = 2 * m
        ref = jnp.broadcast_to(b.reshape(C // blk, blk, W)[:, m - 1:m, :], (C // blk, blk, W)).reshape(C, W)
        later = (t_row % blk) >= m
        qt = jnp.where(later, q * jnp.exp(jnp.minimum(b - ref, 0.0)), 0.0)
        kt = jnp.where(later, 0.0, kk * jnp.exp(jnp.minimum(ref - b, 0.0))).astype(BF16)
        same = (t_row // blk) == (s_col // blk)
        for h in range(N_HEADS):
            sc = _nt((qt * hm[h]).astype(BF16), kt)
            att[h] = att[h] + jnp.where(same, sc, 0.0)
        m = blk
    if C > SUB:
        for h in range(N_HEADS):
            o = o + _dot(att[h].astype(BF16), (v * hm[h]).astype(BF16))

    st = st_ref[...]
    o = o + _nt((q * jnp.exp(b)).astype(BF16), st.astype(BF16))
    b_end = b[C - 1:C, :]
    ke = kk * jnp.exp(b_end - b)
    st_new = st * jnp.exp(b_end) + jnp.where(bd, _tn(v.astype(BF16), ke.astype(BF16)), 0.0)
    st_ref[...] = st_new
    sout_ref[...] = st_new

    ms = _dot(o * o, bd.astype(F32), HIGHEST) * (1.0 / HEAD_W)
    y = o * lax.rsqrt(ms + NORM_EPS) * gain_ref[...] * _silu(ga)
    o_ref[...] = y[:Cb].astype(o_ref.dtype)


def _hgrn(a, lb, gain, s0, l_real):
    B, L, _ = a.shape
    Cb = min(REC_CHUNK, L)
    W = REC_W
    return pl.pallas_call(
        functools.partial(_hgrn_kernel, REC_CHUNK, Cb, l_real),
        grid=(B, L // Cb),
        in_specs=[pl.BlockSpec((None, Cb, 4 * W), lambda b, c: (b, c, 0)),
                  pl.BlockSpec((1, W), lambda b, c: (0, 0)),
                  pl.BlockSpec((1, W), lambda b, c: (0, 0)),
                  pl.BlockSpec((None, W, W), lambda b, c: (b, 0, 0))],
        out_specs=[pl.BlockSpec((None, Cb, W), lambda b, c: (b, c, 0)),
                   pl.BlockSpec((None, W, W), lambda b, c: (b, 0, 0))],
        out_shape=[jax.ShapeDtypeStruct((B, L, W), BF16), jax.ShapeDtypeStruct((B, W, W), F32)],
        scratch_shapes=[pltpu.VMEM((W, W), F32)],
        compiler_params=_params("parallel", "arbitrary"),
        name="hgrn",
    )(a, lb, gain, s0)


def _log_sigmoid(x):
    return jnp.minimum(x, 0.0) - jnp.log1p(jnp.exp(-jnp.abs(x)))


def _mlstm_kernel(C, Cb, l_real, cg_ref, gt_ref, cw_ref, cb_ref, bf_ref, gain_ref, c0_ref, n0_ref, m0_ref,
                  tail0_ref, o_ref, cout_ref, nout_ref, mout_ref, ct_ref, n_ref, m_ref, tail_ref):
    c = pl.program_id(1)

    @pl.when(c == 0)
    def _():
        ct_ref[...] = c0_ref[...]
        n_ref[...] = n0_ref[...]
        m_ref[...] = m0_ref[...]
        tail_ref[...] = tail0_ref[...]

    W = REC_W
    cg = _pad_rows(cg_ref[...], C)
    gt = _pad_rows(gt_ref[...], C)
    valid = (c * C + lax.broadcasted_iota(jnp.int32, (C, 1), 0)) < l_real

    qk_pre = cg[:, 0:2 * W]
    xp = jnp.concatenate([tail_ref[...], qk_pre], axis=0)
    conv = cb_ref[...]
    base = SUB - (CONV_TAPS - 1)
    for j in range(CONV_TAPS):
        conv = conv + cw_ref[j:j + 1, :] * xp[base + j:base + j + C, :]
    tail_ref[...] = qk_pre[C - SUB:C, :]
    qk = _silu(conv)
    qc = qk[:, 0:W]
    kc = qk[:, W:2 * W] * (HEAD_W ** -0.5)
    vc = cg[:, 2 * W:3 * W]
    oc = cg[:, 3 * W:4 * W]

    lf = jnp.where(valid, _log_sigmoid(gt + bf_ref[...]), 0.0)
    li = jnp.where(valid, gt, NEG)
    fcum = _dot(_tril_ones(C), lf, HIGHEST)
    lane_r = lax.broadcasted_iota(jnp.int32, (ROUTE_LANES, W), 0)
    lane_c = lax.broadcasted_iota(jnp.int32, (ROUTE_LANES, W), 1) // HEAD_W
    li_b = _dot(li, (lane_r == lane_c).astype(F32), HIGHEST)
    f_b = _dot(fcum, (lane_r == lane_c + N_HEADS).astype(F32), HIGHEST)
    f_t = fcum.T
    li_t = li.T

    hm = _head_lane_masks(W)
    bd = _block_diag_mask(W, HEAD_W)
    causal = lax.broadcasted_iota(jnp.int32, (C, C), 0) >= lax.broadcasted_iota(jnp.int32, (C, C), 1)
    m_prev = m_ref[...]
    kc_bf = kc.astype(BF16)
    num = jnp.zeros((C, W), F32)
    den = jnp.zeros((C, W), F32)
    cs_b = jnp.zeros((C, W), F32)
    mt_b = jnp.zeros((C, W), F32)
    for h in range(N_HEADS):
        f_col = fcum[:, N_HEADS + h:N_HEADS + h + 1]
        log_d = jnp.where(causal, f_col - f_t[N_HEADS + h:N_HEADS + h + 1, :] + li_t[h:h + 1, :], NEG)
        log_s = f_col + m_prev[:, h * HEAD_W:h * HEAD_W + 1]
        m_t = jnp.maximum(jnp.max(log_d, axis=1, keepdims=True), log_s)
        w = _nt((qc * hm[h]).astype(BF16), kc_bf) * jnp.exp(log_d - m_t)
        num = num + _dot(w.astype(BF16), (vc * hm[h]).astype(BF16))
        den = den + jnp.sum(w, axis=1, keepdims=True) * hm[h]
        cs_b = cs_b + jnp.exp(log_s - m_t) * hm[h]
        mt_b = mt_b + m_t * hm[h]

    ct = ct_ref[...]
    n_row = n_ref[...]
    num = num + cs_b * _nt(qc.astype(BF16), ct.astype(BF16))
    den = den + cs_b * _dot((qc * n_row).astype(BF16), bd.astype(BF16))
    hh = num / jnp.maximum(jnp.abs(den), jnp.exp(-mt_b))

    m_new = mt_b[C - 1:C, :]
    f_end = f_b[C - 1:C, :]
    ws = jnp.exp(f_end - f_b + li_b - m_new)
    decay = jnp.exp(f_end + m_prev - m_new)
    ct_new = ct * decay + jnp.where(bd, _tn((vc * ws).astype(BF16), kc_bf), 0.0)
    n_new = decay * n_row + jnp.sum(ws * kc, axis=0, keepdims=True)
    ct_ref[...] = ct_new
    n_ref[...] = n_new
    m_ref[...] = m_new
    cout_ref[...] = ct_new
    nout_ref[...] = n_new
    mout_ref[...] = m_new

    ms = _dot(hh * hh, bd.astype(F32), HIGHEST) * (1.0 / HEAD_W)
    y = _sigmoid(oc) * (hh * lax.rsqrt(ms + NORM_EPS) * gain_ref[...])
    o_ref[...] = y[:Cb].astype(o_ref.dtype)


def _mlstm(cg, gates, conv_w, conv_b, bias_f, gain, c0, n0, m0, tail0, l_real):
    B, L, _ = cg.shape
    Cb = min(REC_CHUNK, L)
    W = REC_W
    const = lambda shape: pl.BlockSpec(shape, lambda b, c: (0,) * len(shape))
    per_b = lambda shape: pl.BlockSpec((None,) + shape, lambda b, c: (b,) + (0,) * len(shape))
    return pl.pallas_call(
        functools.partial(_mlstm_kernel, REC_CHUNK, Cb, l_real),
        grid=(B, L // Cb),
        in_specs=[pl.BlockSpec((None, Cb, 4 * W), lambda b, c: (b, c, 0)),
                  pl.BlockSpec((None, Cb, ROUTE_LANES), lambda b, c: (b, c, 0)),
                  const((CONV_TAPS, 2 * W)), const((1, 2 * W)), const((1, ROUTE_LANES)), const((1, W)),
                  per_b((W, W)), per_b((1, W)), per_b((1, W)), per_b((SUB, 2 * W))],
        out_specs=[pl.BlockSpec((None, Cb, W), lambda b, c: (b, c, 0)),
                   per_b((W, W)), per_b((1, W)), per_b((1, W))],
        out_shape=[jax.ShapeDtypeStruct((B, L, W), BF16), jax.ShapeDtypeStruct((B, W, W), F32),
                   jax.ShapeDtypeStruct((B, 1, W), F32), jax.ShapeDtypeStruct((B, 1, W), F32)],
        scratch_shapes=[pltpu.VMEM((W, W), F32), pltpu.VMEM((1, W), F32), pltpu.VMEM((1, W), F32),
                        pltpu.VMEM((SUB, 2 * W), F32)],
        compiler_params=_params("parallel", "arbitrary"),
        name="mlstm",
    )(cg, gates, conv_w, conv_b, bias_f, gain, c0, n0, m0, tail0)


def _diff_finish(o1, o2, lam, gain, out_scale):
    o = o1 - lam * o2
    ms = jnp.mean(o * o, axis=-1, keepdims=True)
    return out_scale * (o * lax.rsqrt(ms + NORM_EPS) * gain)


def _attn_kernel(T, out_scale, qi_ref, ki_ref, q_ref, k_ref, v_ref, lam_ref, gain_ref, o_ref,
                 qq_sc, m_sc, l_sc, acc_sc):
    step_id = pl.program_id(1)
    qi = qi_ref[step_id]
    ki = ki_ref[step_id]
    heads = [slice(h * ATT_W, (h + 1) * ATT_W) for h in range(N_HEADS)]
    rb = ATT_ROWS if (2 * T) % ATT_ROWS == 0 else T
    lane_tiles = T // ATT_W

    @pl.when(ki == 0)
    def _():
        lane = lax.broadcasted_iota(jnp.int32, (1, ATT_W), 1)
        for h in range(N_HEADS):
            q = q_ref[:, heads[h]]
            zero = jnp.zeros_like(q)
            qq_sc[h, 0:T, :] = jnp.where(lane < HEAD_W, q, zero)
            qq_sc[h, T:2 * T, :] = jnp.where(lane >= HEAD_W, q, zero)
        m_sc[...] = jnp.full(m_sc.shape, NEG, F32)
        l_sc[...] = jnp.zeros(l_sc.shape, F32)
        acc_sc[...] = jnp.zeros(acc_sc.shape, F32)

    def step(diagonal):
        for h in range(N_HEADS):
            k_h = k_ref[:, heads[h]]
            v_h = v_ref[:, heads[h]]
            for r0 in range(0, 2 * T, rb):
                rows = slice(r0, r0 + rb)
                s = _nt(qq_sc[h, rows, :], k_h)
                if diagonal:
                    row = (r0 + lax.broadcasted_iota(jnp.int32, (rb, 1), 0)) % T
                    col = lax.broadcasted_iota(jnp.int32, (1, T), 1)
                    s = jnp.where(row >= col, s, NEG)
                m_old = m_sc[h, rows, :]
                m_new = jnp.maximum(m_old, jnp.max(s, axis=1, keepdims=True))
                alpha = jnp.exp(m_old - m_new)
                p = jnp.exp(s - jnp.concatenate([m_new] * lane_tiles, axis=1))
                p_lanes = p[:, 0:ATT_W]
                for t in range(1, lane_tiles):
                    p_lanes = p_lanes + p[:, t * ATT_W:(t + 1) * ATT_W]
                l_sc[h, rows, :] = alpha * l_sc[h, rows, :] + p_lanes
                acc_sc[h, rows, :] = alpha * acc_sc[h, rows, :] + _dot(p.astype(BF16), v_h)
                m_sc[h, rows, :] = m_new

    @pl.when(ki < qi)
    def _():
        step(False)

    @pl.when(ki == qi)
    def _():
        step(True)
        for h in range(N_HEADS):
            on = acc_sc[h] / jnp.sum(l_sc[h], axis=1, keepdims=True)
            y = _diff_finish(on[:T], on[T:], lam_ref[...], gain_ref[:, heads[h]], out_scale)
            o_ref[:, heads[h]] = y.astype(o_ref.dtype)


def _attn_prompt(q, k, v, lam_row, gain, out_scale):
    B, L, _ = q.shape
    T = ATT_TILE if L % ATT_TILE == 0 else SEQ_TILE
    n = L // T
    pairs = [(a, b) for a in range(n) for b in range(a + 1)]
    qi_arr = jnp.asarray([a for a, _ in pairs], jnp.int32)
    ki_arr = jnp.asarray([b for _, b in pairs], jnp.int32)
    q_spec = pl.BlockSpec((None, T, ATT_ALL), lambda b, p, qi, ki: (b, qi[p], 0))
    kv_spec = pl.BlockSpec((None, T, ATT_ALL), lambda b, p, qi, ki: (b, ki[p], 0))
    grid_spec = pltpu.PrefetchScalarGridSpec(
        num_scalar_prefetch=2,
        grid=(B, len(pairs)),
        in_specs=[q_spec, kv_spec, kv_spec,
                  pl.BlockSpec((1, ATT_W), lambda b, p, qi, ki: (0, 0)),
                  pl.BlockSpec((1, ATT_ALL), lambda b, p, qi, ki: (0, 0))],
        out_specs=q_spec,
        scratch_shapes=[pltpu.VMEM((N_HEADS, 2 * T, ATT_W), BF16), pltpu.VMEM((N_HEADS, 2 * T, ATT_W), F32),
                        pltpu.VMEM((N_HEADS, 2 * T, ATT_W), F32), pltpu.VMEM((N_HEADS, 2 * T, ATT_W), F32)],
    )
    return pl.pallas_call(
        functools.partial(_attn_kernel, T, out_scale),
        grid_spec=grid_spec,
        out_shape=jax.ShapeDtypeStruct((B, L, ATT_ALL), BF16),
        compiler_params=_params("parallel", "arbitrary"),
        name="attn_prompt",
    )(qi_arr, ki_arr, q, k, v, lam_row, gain)


def _dec_attn_kernel(npg, ls, page, out_scale, pt_ref, q_ref, kn_ref, vn_ref, lam_ref, gain_ref, *refs):
    del pt_ref
    k_refs, v_refs = refs[:npg], refs[npg:2 * npg]
    o_ref, qs_sc, m_sc, l_sc, acc_sc = refs[2 * npg:]
    j = pl.program_id(1)
    rows = 2 * N_HEADS * ls
    cols = page * N_HEADS
    row_head = lax.broadcasted_iota(jnp.int32, (rows, 1), 0) // (2 * ls)

    @pl.when(j == 0)
    def _():
        q = q_ref[...].astype(F32)
        lane = lax.broadcasted_iota(jnp.int32, (1, ATT_W), 1)
        parts = []
        for h in range(N_HEADS):
            qh = q[:, h * ATT_W:(h + 1) * ATT_W]
            parts += [jnp.where(lane < HEAD_W, qh, 0.0), jnp.where(lane >= HEAD_W, qh, 0.0)]
        qs_sc[...] = jnp.concatenate(parts, axis=0).astype(BF16)
        m_sc[...] = jnp.full(m_sc.shape, NEG, F32)
        l_sc[...] = jnp.zeros(l_sc.shape, F32)
        acc_sc[...] = jnp.zeros(acc_sc.shape, F32)

    qs = qs_sc[...]

    def update(scores, values):
        m_old = m_sc[...]
        m_new = m_old
        for s in scores:
            m_new = jnp.maximum(m_new, jnp.max(s, axis=1, keepdims=True))
        alpha = jnp.exp(m_old - m_new)
        l_new = alpha * l_sc[...]
        acc = alpha * acc_sc[...]
        for s, val in zip(scores, values):
            p = jnp.exp(s - m_new)
            l_new = l_new + jnp.sum(p, axis=1, keepdims=True)
            acc = acc + _dot(p.astype(BF16), val)
        l_sc[...] = l_new
        acc_sc[...] = acc
        m_sc[...] = m_new

    own_head = (lax.broadcasted_iota(jnp.int32, (1, cols), 1) % N_HEADS) == row_head
    update([jnp.where(own_head, _nt(qs, kr[...].astype(BF16)), NEG) for kr in k_refs],
           [vr[...].astype(BF16) for vr in v_refs])

    @pl.when(j == pl.num_programs(1) - 1)
    def _():
        n_new = ls * N_HEADS
        kn = _pad_rows(kn_ref[...], ATT_W).astype(BF16)
        vn = _pad_rows(vn_ref[...], ATT_W).astype(BF16)
        col = lax.broadcasted_iota(jnp.int32, (1, ATT_W), 1)
        q_idx = lax.broadcasted_iota(jnp.int32, (rows, 1), 0) % ls
        ok = (col % N_HEADS == row_head) & (col // N_HEADS <= q_idx) & (col < n_new)
        update([jnp.where(ok, _nt(qs, kn), NEG)], [vn])
        on = acc_sc[...] / l_sc[...]
        gain = gain_ref[...]
        parts = []
        for h in range(N_HEADS):
            r0 = h * 2 * ls
            parts.append(_diff_finish(on[r0:r0 + ls], on[r0 + ls:r0 + 2 * ls], lam_ref[...],
                                      gain[:, h * ATT_W:(h + 1) * ATT_W], out_scale))
        o_ref[...] = jnp.concatenate(parts, axis=1).astype(o_ref.dtype)


def _attn_sample(q, k_new, v_new, cache_k, cache_v, page_table, layer, lam_row, gain, out_scale):
    B, ls, _ = q.shape
    n_pages = page_table.shape[1]
    page = cache_k.shape[2]
    cols = page * N_HEADS
    npg = PAGES_PER_STEP if n_pages % PAGES_PER_STEP == 0 else 1
    ck = cache_k.reshape(cache_k.shape[0], cache_k.shape[1], cols, ATT_W)
    cv = cache_v.reshape(cache_v.shape[0], cache_v.shape[1], cols, ATT_W)
    kn = k_new.reshape(B, ls * N_HEADS, ATT_W)
    vn = v_new.reshape(B, ls * N_HEADS, ATT_W)

    def page_spec(i):
        return pl.BlockSpec((None, None, cols, ATT_W), lambda b, j, pt: (layer, pt[b, j * npg + i], 0, 0))

    per_b = pl.BlockSpec((None, ls, ATT_ALL), lambda b, j, pt: (b, 0, 0))
    new_spec = pl.BlockSpec((None, ls * N_HEADS, ATT_W), lambda b, j, pt: (b, 0, 0))
    rows = 2 * N_HEADS * ls
    grid_spec = pltpu.PrefetchScalarGridSpec(
        num_scalar_prefetch=1,
        grid=(B, n_pages // npg),
        in_specs=[per_b, new_spec, new_spec,
                  pl.BlockSpec((1, ATT_W), lambda b, j, pt: (0, 0)),
                  pl.BlockSpec((1, ATT_ALL), lambda b, j, pt: (0, 0))]
                 + [page_spec(i) for i in range(npg)] * 2,
        out_specs=per_b,
        scratch_shapes=[pltpu.VMEM((rows, ATT_W), BF16), pltpu.VMEM((rows, 1), F32), pltpu.VMEM((rows, 1), F32),
                        pltpu.VMEM((rows, ATT_W), F32)],
    )
    return pl.pallas_call(
        functools.partial(_dec_attn_kernel, npg, ls, page, out_scale),
        grid_spec=grid_spec,
        out_shape=jax.ShapeDtypeStruct((B, ls, ATT_ALL), BF16),
        compiler_params=_params("parallel", "arbitrary"),
        name="attn_sample",
    )(page_table, q, kn, vn, lam_row, gain, *([ck] * npg), *([cv] * npg))


def _outproj_kernel(alpha, ma_ref, mb_ref, mc_ref, h_ref, wo_ref, g_ref, b_ref, wr_ref, br_ref,
                    h1_ref, eid_ref, gate_ref):
    W = REC_W
    mix = (_dot(ma_ref[...], wo_ref[0:W, :]) + _dot(mb_ref[...], wo_ref[W:W + ATT_ALL, :])
           + _dot(mc_ref[...], wo_ref[W + ATT_ALL:2 * W + ATT_ALL, :]))
    h1 = _layernorm(alpha * h_ref[...] + mix, g_ref[...], b_ref[...])
    h1_ref[...] = h1

    logits = _dot(h1.astype(BF16), wr_ref[...]) + br_ref[...]
    lane = lax.broadcasted_iota(jnp.int32, (1, ROUTE_LANES), 1)
    lane_f = lane.astype(F32)
    far = float(ROUTE_LANES)

    gl = jnp.where(lane < N_GROUPS, logits, NEG)
    g_max = jnp.max(gl, axis=1, keepdims=True)
    g_top = 1.0 / jnp.sum(jnp.exp(gl - g_max), axis=1, keepdims=True)
    g_idx = jnp.min(jnp.where(gl == g_max, lane_f, far), axis=1, keepdims=True)

    e_lane = lane_f - float(EXPERT_LANE0)
    in_group = (e_lane >= g_idx * GROUP_EXPERTS) & (e_lane < (g_idx + 1.0) * GROUP_EXPERTS)
    el = jnp.where(in_group, logits, NEG)
    e_max = jnp.max(el, axis=1, keepdims=True)
    e_sum = jnp.sum(jnp.exp(el - e_max), axis=1, keepdims=True)
    idx1 = jnp.min(jnp.where(el == e_max, lane_f, far), axis=1, keepdims=True)
    el2 = jnp.where(lane_f == idx1, NEG, el)
    e_max2 = jnp.max(el2, axis=1, keepdims=True)
    idx2 = jnp.min(jnp.where(el2 == e_max2, lane_f, far), axis=1, keepdims=True)
    p1 = 1.0 / e_sum
    p2 = jnp.exp(e_max2 - e_max) / e_sum
    tot = p1 + p2
    gate_ref[...] = jnp.where(lane == 0, g_top * (p1 / tot), jnp.where(lane == 1, g_top * (p2 / tot), 0.0))
    eid = jnp.where(lane == 0, idx1 - float(EXPERT_LANE0), jnp.where(lane == 1, idx2 - float(EXPERT_LANE0), 0.0))
    eid_ref[...] = eid.astype(jnp.int32)


def _outproj(ma, mb, mc, h, wo, g, b, wr, br, alpha):
    T, D = h.shape
    tm = min(ROW_TILE, T)
    tile = lambda n: pl.BlockSpec((tm, n), lambda i: (i, 0))
    const = lambda r, n: pl.BlockSpec((r, n), lambda i: (0, 0))
    return pl.pallas_call(
        functools.partial(_outproj_kernel, alpha),
        grid=(T // tm,),
        in_specs=[tile(REC_W), tile(ATT_ALL), tile(REC_W), tile(D), const(2 * REC_W + ATT_ALL, D),
                  const(1, D), const(1, D), const(D, ROUTE_LANES), const(1, ROUTE_LANES)],
        out_specs=[tile(D), tile(ROUTE_LANES), tile(ROUTE_LANES)],
        out_shape=[jax.ShapeDtypeStruct((T, D), F32), jax.ShapeDtypeStruct((T, ROUTE_LANES), jnp.int32),
                   jax.ShapeDtypeStruct((T, ROUTE_LANES), F32)],
        compiler_params=_params("parallel"),
        name="outproj",
    )(ma, mb, mc, h, wo, g, b, wr, br)


def _expert_kernel(row_ref, blke_ref, nval_ref, x_hbm, wg_ref, wu_ref, wd_ref, out_hbm,
                   xbuf, ybuf, gsem, ssem):
    del blke_ref
    i = pl.program_id(0)
    nblk = pl.num_programs(0)
    cur = i % 2
    n_groups = MOE_ROWS // SUB

    def start_gather(blk, buf):
        @pl.when(nval_ref[blk] > 0)
        def _():
            def group(g, carry):
                base = blk * MOE_ROWS + g * SUB
                for u in range(SUB):
                    tok = lax.shift_right_logical(row_ref[base + u], 1)
                    pltpu.make_async_copy(x_hbm.at[lax.shift_right_logical(tok, 3), pl.ds(tok & (SUB - 1), 1), :],
                                          xbuf.at[buf, g, pl.ds(u, 1), :], gsem.at[buf]).start()
                return carry
            lax.fori_loop(0, n_groups, group, 0)

    def scatter_copy(buf, g, u, code):
        tok = lax.shift_right_logical(code, 1)
        return pltpu.make_async_copy(ybuf.at[buf, g, pl.ds(u, 1), :],
                                     out_hbm.at[code & 1, lax.shift_right_logical(tok, 3), pl.ds(tok & (SUB - 1), 1), :],
                                     ssem.at[buf])

    def start_scatter(blk, buf):
        n = nval_ref[blk]
        full = lax.shift_right_logical(n, 3)

        def group(g, carry):
            base = blk * MOE_ROWS + g * SUB
            for u in range(SUB):
                scatter_copy(buf, g, u, row_ref[base + u]).start()
            return carry
        lax.fori_loop(0, full, group, 0)

        base = blk * MOE_ROWS + full * SUB
        for u in range(SUB - 1):
            @pl.when(full * SUB + u < n)
            def _():
                scatter_copy(buf, full, u, row_ref[base + u]).start()

    def wait_scatter(blk, buf):
        n = nval_ref[blk]
        full = lax.shift_right_logical(n, 3)

        @pl.when(full > 0)
        def _():
            pltpu.make_async_copy(ybuf.at[buf, pl.ds(0, full)], out_hbm.at[0, pl.ds(0, full)], ssem.at[buf]).wait()

        for u in range(SUB - 1):
            @pl.when(full * SUB + u < n)
            def _():
                scatter_copy(buf, full, u, 0).wait()

    @pl.when(i == 0)
    def _():
        start_gather(0, 0)

    @pl.when(i + 1 < nblk)
    def _():
        start_gather(i + 1, 1 - cur)

    @pl.when(i >= 2)
    def _():
        wait_scatter(i - 2, cur)

    @pl.when(nval_ref[i] > 0)
    def _():
        pltpu.make_async_copy(x_hbm.at[pl.ds(0, n_groups)], xbuf.at[cur], gsem.at[cur]).wait()
        d = xbuf.shape[-1]
        xb = xbuf[cur].reshape(MOE_ROWS, d).astype(BF16)
        hid = _silu(_dot(xb, wg_ref[...])) * _dot(xb, wu_ref[...])
        ybuf[cur] = _dot(hid.astype(BF16), wd_ref[...]).reshape(n_groups, SUB, d)
        start_scatter(i, cur)

    @pl.when(i == nblk - 1)
    def _():
        @pl.when(i >= 1)
        def _():
            wait_scatter(i - 1, 1 - cur)
        wait_scatter(i, cur)


def _experts(x, row_code, blk_e, blk_nval, wg, wu, wd):
    T, D = x.shape
    n_blocks = blk_e.shape[0]
    de = wg.shape[2]
    grid_spec = pltpu.PrefetchScalarGridSpec(
        num_scalar_prefetch=3,
        grid=(n_blocks,),
        in_specs=[pl.BlockSpec(memory_space=pl.ANY),
                  pl.BlockSpec((None, D, de), lambda i, code, be, nv: (be[i], 0, 0)),
                  pl.BlockSpec((None, D, de), lambda i, code, be, nv: (be[i], 0, 0)),
                  pl.BlockSpec((None, de, D), lambda i, code, be, nv: (be[i], 0, 0))],
        out_specs=pl.BlockSpec(memory_space=pl.ANY),
        scratch_shapes=[pltpu.VMEM((2, MOE_ROWS // SUB, SUB, D), F32), pltpu.VMEM((2, MOE_ROWS // SUB, SUB, D), F32),
                        pltpu.SemaphoreType.DMA((2,)), pltpu.SemaphoreType.DMA((2,))],
    )
    slots = pl.pallas_call(
        _expert_kernel,
        grid_spec=grid_spec,
        out_shape=jax.ShapeDtypeStruct((2, T // SUB, SUB, D), F32),
        compiler_params=_params("arbitrary"),
        name="experts",
    )(row_code, blk_e, blk_nval, x.reshape(T // SUB, SUB, D), wg, wu, wd)
    return slots.reshape(2, T, D)


def _dispatch(eid, T):
    K = 2
    A = T * K
    flat_e = eid[:, :K].reshape(-1)
    onehot = (flat_e[:, None] == jnp.arange(N_EXPERTS, dtype=jnp.int32)[None, :]).astype(F32)
    a_pad = -(-A // MOE_ROWS) * MOE_ROWS
    oh3 = jnp.pad(onehot, ((0, a_pad - A), (0, 0))).reshape(a_pad // MOE_ROWS, MOE_ROWS, N_EXPERTS)
    tril = jnp.tril(jnp.ones((MOE_ROWS, MOE_ROWS), F32))
    within = jnp.einsum('ts,bse->bte', tril, oh3, precision=HIGHEST)
    blk_tot = within[:, -1, :]
    before = jnp.cumsum(blk_tot, axis=0) - blk_tot
    rank = jnp.sum((within + before[:, None, :]) * oh3, axis=-1).reshape(a_pad)[:A] - 1.0
    counts = jnp.sum(blk_tot, axis=0).astype(jnp.int32)
    padded = (counts + MOE_ROWS - 1) // MOE_ROWS * MOE_ROWS
    pad_end = jnp.cumsum(padded)
    pad_start = pad_end - padded
    dest = jnp.sum(onehot * pad_start.astype(F32)[None, :], axis=-1) + rank
    n_blocks = -(-A // MOE_ROWS) + N_EXPERTS
    R = n_blocks * MOE_ROWS
    row_code = jnp.zeros((R,), jnp.int32).at[dest.astype(jnp.int32)].set(jnp.arange(A, dtype=jnp.int32))
    blk_start = jnp.arange(n_blocks, dtype=jnp.int32) * MOE_ROWS
    blk_e = jnp.minimum(jnp.sum((pad_end[None, :] <= blk_start[:, None]).astype(jnp.int32), axis=1), N_EXPERTS - 1)
    blk_nval = jnp.clip((pad_start + counts)[blk_e] - blk_start, 0, MOE_ROWS).astype(jnp.int32)
    return row_code, blk_e.astype(jnp.int32), blk_nval


def _combine_kernel(alpha, h_ref, y_ref, gate_ref, g_ref, b_ref, o_ref):
    gate = gate_ref[...]
    y = gate[:, 0:1] * y_ref[0] + gate[:, 1:2] * y_ref[1]
    o_ref[...] = _layernorm(alpha * h_ref[...] + y, g_ref[...], b_ref[...])


def _combine(h1, slots, gates, g, b, alpha):
    T, D = h1.shape
    tm = min(ROW_TILE, T)
    return pl.pallas_call(
        functools.partial(_combine_kernel, alpha),
        grid=(T // tm,),
        in_specs=[pl.BlockSpec((tm, D), lambda i: (i, 0)), pl.BlockSpec((2, tm, D), lambda i: (0, i, 0)),
                  pl.BlockSpec((tm, ROUTE_LANES), lambda i: (i, 0)),
                  pl.BlockSpec((1, D), lambda i: (0, 0)), pl.BlockSpec((1, D), lambda i: (0, 0))],
        out_specs=pl.BlockSpec((tm, D), lambda i: (i, 0)),
        out_shape=jax.ShapeDtypeStruct((T, D), F32),
        compiler_params=_params("parallel"),
        name="combine",
    )(h1, slots, gates, g, b)


def _block_diag(blocks):
    B, H, n, _ = blocks.shape
    eye = jnp.eye(H, dtype=blocks.dtype)
    return jnp.einsum('bhij,hg->bhigj', blocks, eye).reshape(B, H * n, H * n)


def _diag_blocks(mat, H):
    B, n = mat.shape[0], mat.shape[1] // H
    m5 = mat.reshape(B, H, n, H, n)
    return jnp.stack([m5[:, h, :, h, :] for h in range(H)], axis=1)


def _layer(x, do_ln, B, L, l_real, lw, state, past):
    outs = _inproj(x, lw['ln_g'], lw['ln_b'], lw['w_in'], lw['b_in'], do_ln)
    a, qb, kb, vb, cg, gates, kb16, vb16 = outs[:8]
    h = outs[8] if do_ln else x
    s3 = lambda z: z.reshape(B, L, z.shape[-1])

    mix_a, st_a = _hgrn(s3(a), lw['lb'], lw['norm_a'], state['hgrn'], l_real)
    mix_c, ct, n_c, m_c = _mlstm(s3(cg), s3(gates), lw['conv_w'], lw['conv_b'], lw['b_f'], lw['norm_c'],
                                 state['c'], state['n'], state['m'], state['tail'], l_real)
    if past is None:
        mix_b = _attn_prompt(s3(qb), s3(kb16), s3(vb16), lw['lam'], lw['norm_b'], lw['out_scale'])
    else:
        mix_b = _attn_sample(s3(qb), s3(kb), s3(vb), past[0], past[1], past[2], lw['layer'], lw['lam'],
                             lw['norm_b'], lw['out_scale'])

    flat = lambda z: z.reshape(B * L, z.shape[-1])
    h1, eid, gate = _outproj(flat(mix_a), flat(mix_b), flat(mix_c), h, lw['w_out'], lw['ln1_g'], lw['ln1_b'],
                             lw['w_route'], lw['b_route'], lw['alpha'])
    row_code, blk_e, blk_nval = _dispatch(eid, B * L)
    slots = _experts(h1, row_code, blk_e, blk_nval, lw['w_eg'], lw['w_eu'], lw['w_ed'])
    h2 = _combine(h1, slots, gate, lw['ln2_g'], lw['ln2_b'], lw['alpha'])

    new_state = {
        'k': s3(kb), 'v': s3(vb),
        'hgrn': jnp.swapaxes(_diag_blocks(st_a, N_HEADS), -1, -2),
        'c': _diag_blocks(ct, N_HEADS),
        'n': n_c.reshape(B, N_HEADS, HEAD_W),
        'm': m_c.reshape(B, N_HEADS, HEAD_W)[:, :, 0],
        'conv': s3(cg)[:, l_real - (CONV_TAPS - 1):l_real, :2 * REC_W],
    }
    return h2, new_state


def kernel(x_prompt, x_sample, cache_k, cache_v, page_table, state_hgrn, state_mlstm_c, state_mlstm_n, state_mlstm_m, state_mlstm_conv, meta_tokens, ln_emb_g, ln_emb_b, w_in, b_in, b_mlstm_f, conv_w, conv_b, hgrn_lb_logits, lambda_q1, lambda_k1, lambda_q2, lambda_k2, norm_a, norm_b, norm_c, w_out, ln1_g, ln1_b, w_router_group, b_router_group, w_router_expert, b_router_expert, w_exp_gate, w_exp_up, w_exp_down, ln2_g, ln2_b):
    depth = w_in.shape[0]
    Bp, Lq, D = x_prompt.shape
    Bs, Ls, _ = x_sample.shape
    n_meta = meta_tokens.shape[0]
    l_real = n_meta + Lq
    Lp = -(-l_real // SEQ_TILE) * SEQ_TILE
    alpha = (2.0 * depth) ** 0.25
    row = lambda z: z.reshape(1, -1).astype(F32)

    meta = jnp.broadcast_to(meta_tokens[None].astype(x_prompt.dtype), (Bp, n_meta, D))
    xp = jnp.concatenate([meta, x_prompt, jnp.zeros((Bp, Lp - l_real, D), x_prompt.dtype)], axis=1)
    hp = xp.reshape(Bp * Lp, D)
    hs = x_sample.reshape(Bs * Ls, D)

    lb_sm = jax.nn.softmax(hgrn_lb_logits.astype(F32), axis=0)
    lb_all = jnp.cumsum(lb_sm, axis=0) - lb_sm[0]

    zeros_p = {
        'hgrn': jnp.zeros((Bp, REC_W, REC_W), F32), 'c': jnp.zeros((Bp, REC_W, REC_W), F32),
        'n': jnp.zeros((Bp, 1, REC_W), F32), 'm': jnp.zeros((Bp, 1, REC_W), F32),
        'tail': jnp.zeros((Bp, SUB, 2 * REC_W), F32),
    }

    res_p, res_s = [], []
    for l in range(depth):
        lam_init = 0.8 - 0.6 * math.exp(-0.3 * l)
        lam = (jnp.exp(jnp.sum(lambda_q1[l].astype(F32) * lambda_k1[l].astype(F32)))
               - jnp.exp(jnp.sum(lambda_q2[l].astype(F32) * lambda_k2[l].astype(F32))) + lam_init)
        pad_cols = N_PROJ_PAD - N_PROJ_RAW
        bias_f = jnp.zeros((ROUTE_LANES,), F32).at[N_HEADS:2 * N_HEADS].set(b_mlstm_f[l].astype(F32))
        w_route = jnp.concatenate([w_router_group[l], w_router_expert[l]], axis=1)
        b_route = jnp.concatenate([b_router_group[l], b_router_expert[l]])
        r_pad = ROUTE_LANES - w_route.shape[1]
        lw = {
            'layer': l, 'alpha': alpha, 'out_scale': 1.0 - lam_init,
            'ln_g': row(ln_emb_g), 'ln_b': row(ln_emb_b),
            'w_in': jnp.pad(w_in[l], ((0, 0), (0, pad_cols))).astype(BF16),
            'b_in': row(jnp.pad(b_in[l], (0, pad_cols))),
            'lb': row(lb_all[l]), 'norm_a': row(norm_a[l]), 'norm_b': row(norm_b[l]), 'norm_c': row(norm_c[l]),
            'lam': jnp.full((1, ATT_W), lam, F32),
            'conv_w': conv_w[l].astype(F32), 'conv_b': row(conv_b[l]), 'b_f': row(bias_f),
            'w_out': w_out[l].astype(BF16), 'ln1_g': row(ln1_g[l]), 'ln1_b': row(ln1_b[l]),
            'w_route': jnp.pad(w_route, ((0, 0), (0, r_pad))).astype(BF16), 'b_route': row(jnp.pad(b_route, (0, r_pad))),
            'w_eg': w_exp_gate[l].astype(BF16), 'w_eu': w_exp_up[l].astype(BF16), 'w_ed': w_exp_down[l].astype(BF16),
            'ln2_g': row(ln2_g[l]), 'ln2_b': row(ln2_b[l]),
        }
        hp, st_p = _layer(hp, l == 0, Bp, Lp, l_real, lw, zeros_p, None)

        state_s = {
            'hgrn': _block_diag(jnp.swapaxes(state_hgrn[l].astype(F32), -1, -2)),
            'c': _block_diag(state_mlstm_c[l].astype(F32)),
            'n': state_mlstm_n[l].astype(F32).reshape(Bs, 1, REC_W),
            'm': jnp.repeat(state_mlstm_m[l].astype(F32), HEAD_W, axis=-1).reshape(Bs, 1, REC_W),
            'tail': jnp.pad(state_mlstm_conv[l].astype(F32), ((0, 0), (SUB - (CONV_TAPS - 1), 0), (0, 0))),
        }
        hs, st_s = _layer(hs, l == 0, Bs, Ls, Ls, lw, state_s, (cache_k, cache_v, page_table))
        res_p.append(st_p)
        res_s.append(st_s)

    y_prompt = hp.reshape(Bp, Lp, D)[:, n_meta:l_real]
    y_sample = hs.reshape(Bs, Ls, D)
    stack = lambda res, key, f=lambda z: z: jnp.stack([f(r[key]) for r in res])
    kv_p = lambda z: z[:, :l_real].reshape(Bp, l_real, N_HEADS, ATT_W)
    kv_s = lambda z: z.reshape(Bs, Ls, N_HEADS, ATT_W)
    return (y_prompt, y_sample,
            stack(res_p, 'k', kv_p), stack(res_p, 'v', kv_p), stack(res_s, 'k', kv_s), stack(res_s, 'v', kv_s),
            stack(res_p, 'hgrn'), stack(res_s, 'hgrn'),
            stack(res_p, 'c'), stack(res_s, 'c'),
            stack(res_p, 'n'), stack(res_s, 'n'),
            stack(res_p, 'm'), stack(res_s, 'm'),
            stack(res_p, 'conv'), stack(res_s, 'conv'))
```

```python
import functools
import math

import jax
import jax.numpy as jnp
from jax import lax
from jax.experimental import pallas as pl
from jax.experimental.pallas import tpu as pltpu

F32 = jnp.float32
BF16 = jnp.bfloat16
HIGHEST = lax.Precision.HIGHEST

N_META = 16
N_HEADS = 4
HEAD_W = 64
ATT_W = 2 * HEAD_W
REC_W = N_HEADS * HEAD_W
ATT_ALL = N_HEADS * ATT_W
CONV_TAPS = 4
N_GROUPS = 4
GROUP_EXPERTS = 8
N_EXPERTS = N_GROUPS * GROUP_EXPERTS
LANES = 128
ROUTE_LANES = LANES
EXPERT_LANE0 = N_GROUPS
MOE_ROWS = 128
LN_EPS = 1e-5
NORM_EPS = 1e-6
NEG = -1e30

SEQ_TILE = 128
REC_CHUNK = 128
SUB = 8
ATT_TILE = 384
ATT_ROWS = 768
PAGES_PER_STEP = 8
ROW_TILE = 256
VMEM_LIMIT = 48 * 1024 * 1024

IN_WIDTHS = (4 * REC_W, ATT_ALL, ATT_ALL, ATT_ALL, 4 * REC_W, ROUTE_LANES)
N_PROJ_RAW = 4 * REC_W + 3 * ATT_ALL + 4 * REC_W + 2 * N_HEADS
N_PROJ_PAD = sum(IN_WIDTHS)


def _nt(a, b):
    return lax.dot_general(a, b, (((1,), (1,)), ((), ())), preferred_element_type=F32)


def _tn(a, b):
    return lax.dot_general(a, b, (((0,), (0,)), ((), ())), preferred_element_type=F32)


def _dot(a, b, precision=None):
    return jnp.dot(a, b, preferred_element_type=F32, precision=precision)


def _layernorm(x, g, b):
    mu = jnp.mean(x, axis=-1, keepdims=True)
    xc = x - mu
    var = jnp.mean(xc * xc, axis=-1, keepdims=True)
    return xc * lax.rsqrt(var + LN_EPS) * g + b


def _sigmoid(x):
    return 1.0 / (1.0 + jnp.exp(-x))


def _silu(x):
    return x * _sigmoid(x)


def _head_lane_masks(width):
    lane = lax.broadcasted_iota(jnp.int32, (1, width), 1)
    per = width // N_HEADS
    return [(lane // per == h).astype(F32) for h in range(N_HEADS)]


def _block_diag_mask(n, blk):
    r = lax.broadcasted_iota(jnp.int32, (n, n), 0) // blk
    c = lax.broadcasted_iota(jnp.int32, (n, n), 1) // blk
    return r == c


def _tril_ones(n):
    r = lax.broadcasted_iota(jnp.int32, (n, n), 0)
    c = lax.broadcasted_iota(jnp.int32, (n, n), 1)
    return (r >= c).astype(F32)


def _to_row_tiles(ref_at, x):
    for c in range(x.shape[1] // LANES):
        ref_at[:, c, :] = x[:, c * LANES:(c + 1) * LANES]


def _from_row_tiles(ref_at):
    return jnp.concatenate([ref_at[:, c, :] for c in range(ref_at.shape[1])], axis=1)


def _params(*sem):
    return pltpu.CompilerParams(dimension_semantics=sem, vmem_limit_bytes=VMEM_LIMIT)


def _inproj_kernel(do_ln, x_ref, g_ref, b_ref, w_ref, bias_ref, *outs):
    x = x_ref[...]
    if do_ln:
        x = _layernorm(x, g_ref[...], b_ref[...])
        outs[-1][...] = x
    xb = x.astype(BF16)
    col = 0
    for idx, width in enumerate(IN_WIDTHS):
        acc = _dot(xb, w_ref[:, col:col + width]) + bias_ref[:, col:col + width]
        if idx == 1:
            acc = acc * (HEAD_W ** -0.5)
        if idx in (2, 3):
            _to_row_tiles(outs[idx], acc)
            outs[len(IN_WIDTHS) + idx - 2][...] = acc.astype(BF16)
        else:
            outs[idx][...] = acc.astype(outs[idx].dtype)
        col += width


def _inproj(x, ln_g, ln_b, w, bias, do_ln):
    T, D = x.shape
    tm = min(ROW_TILE, T)
    widths = IN_WIDTHS + (ATT_ALL, ATT_ALL)
    dtypes = (F32, BF16, F32, F32, F32, F32, BF16, BF16)
    out_shape = [jax.ShapeDtypeStruct((T, wd), dt) for wd, dt in zip(widths, dtypes)]
    out_specs = [pl.BlockSpec((tm, wd), lambda i: (i, 0)) for wd in widths]
    for idx in (2, 3):
        out_shape[idx] = jax.ShapeDtypeStruct((T, N_HEADS, ATT_W), F32)
        out_specs[idx] = pl.BlockSpec((tm, N_HEADS, ATT_W), lambda i: (i, 0, 0))
    if do_ln:
        out_shape.append(jax.ShapeDtypeStruct((T, D), F32))
        out_specs.append(pl.BlockSpec((tm, D), lambda i: (i, 0)))
    row = lambda n: pl.BlockSpec((1, n), lambda i: (0, 0))
    return pl.pallas_call(
        functools.partial(_inproj_kernel, do_ln),
        grid=(T // tm,),
        in_specs=[pl.BlockSpec((tm, D), lambda i: (i, 0)), row(D), row(D),
                  pl.BlockSpec((D, N_PROJ_PAD), lambda i: (0, 0)), row(N_PROJ_PAD)],
        out_specs=out_specs,
        out_shape=out_shape,
        compiler_params=_params("parallel"),
        name="inproj",
    )(x, ln_g, ln_b, w, bias)


def _pad_rows(a, rows):
    if a.shape[0] == rows:
        return a
    return jnp.concatenate([a, jnp.zeros((rows - a.shape[0], a.shape[1]), a.dtype)], axis=0)


def _hgrn_kernel(C, Cb, l_real, a_ref, lb_ref, gain_ref, s0_ref, o_ref, sout_ref, st_ref):
    c = pl.program_id(1)

    @pl.when(c == 0)
    def _():
        st_ref[...] = s0_ref[...]

    W = REC_W
    a = _pad_rows(a_ref[...], C)
    q, fa, v, ga = a[:, 0:W], a[:, W:2 * W], a[:, 2 * W:3 * W], a[:, 3 * W:4 * W]
    lb = lb_ref[...]
    valid = (c * C + lax.broadcasted_iota(jnp.int32, (C, 1), 0)) < l_real
    f = lb + (1.0 - lb) * _sigmoid(fa)
    g = jnp.where(valid, jnp.log(f), 0.0)
    kk = jnp.where(valid, 1.0 - f, 0.0)
    b = _dot(_tril_ones(C), g, HIGHEST)

    bd = _block_diag_mask(W, HEAD_W)
    bd_bf = bd.astype(BF16)
    hm = _head_lane_masks(W)

    nb = C // SUB
    b3, q3, k3, v3 = (z.reshape(nb, SUB, W) for z in (b, q, kk, v))
    t_in = lax.broadcasted_iota(jnp.int32, (nb, SUB, W), 1)
    o3 = jnp.zeros((nb, SUB, W), F32)
    for s in range(SUB):
        e = jnp.exp(jnp.minimum(b3 - b3[:, s:s + 1, :], 0.0))
        p = jnp.where(t_in >= s, e * q3 * k3[:, s:s + 1, :], 0.0)
        r = _dot(p.reshape(C, W).astype(BF16), bd_bf)
        o3 = o3 + r.reshape(nb, SUB, W) * v3[:, s:s + 1, :]
    o = o3.reshape(C, W)

    t_row = lax.broadcasted_iota(jnp.int32, (C, 1), 0)
    s_col = lax.broadcasted_iota(jnp.int32, (1, C), 1)
    att = [jnp.zeros((C, C), F32) for _ in range(N_HEADS)]
    m = SUB
    while m < C:
        blk = 2 * m
        ref = jnp.broadcast_to(b.reshape(C // blk, blk, W)[:, m - 1:m, :], (C // blk, blk, W)).reshape(C, W)
        later = (t_row % blk) >= m
        qt = jnp.where(later, q * jnp.exp(jnp.minimum(b - ref, 0.0)), 0.0)
        kt = jnp.where(later, 0.0, kk * jnp.exp(jnp.minimum(ref - b, 0.0))).astype(BF16)
        same = (t_row // blk) == (s_col // blk)
        for h in range(N_HEADS):
            sc = _nt((qt * hm[h]).astype(BF16), kt)
            att[h] = att[h] + jnp.where(same, sc, 0.0)
        m = blk
    if C > SUB:
        for h in range(N_HEADS):
            o = o + _dot(att[h].astype(BF16), (v * hm[h]).astype(BF16))

    st = st_ref[...]
    o = o + _nt((q * jnp.exp(b)).astype(BF16), st.astype(BF16))
    b_end = b[C - 1:C, :]
    ke = kk * jnp.exp(b_end - b)
    st_new = st * jnp.exp(b_end) + jnp.where(bd, _tn(v.astype(BF16), ke.astype(BF16)), 0.0)
    st_ref[...] = st_new
    sout_ref[...] = st_new

    ms = _dot(o * o, bd.astype(F32), HIGHEST) * (1.0 / HEAD_W)
    y = o * lax.rsqrt(ms + NORM_EPS) * gain_ref[...] * _silu(ga)
    o_ref[...] = y[:Cb].astype(o_ref.dtype)


def _hgrn(a, lb, gain, s0, l_real):
    B, L, _ = a.shape
    Cb = min(REC_CHUNK, L)
    W = REC_W
    return pl.pallas_call(
        functools.partial(_hgrn_kernel, REC_CHUNK, Cb, l_real),
        grid=(B, L // Cb),
        in_specs=[pl.BlockSpec((None, Cb, 4 * W), lambda b, c: (b, c, 0)),
                  pl.BlockSpec((1, W), lambda b, c: (0, 0)),
                  pl.BlockSpec((1, W), lambda b, c: (0, 0)),
                  pl.BlockSpec((None, W, W), lambda b, c: (b, 0, 0))],
        out_specs=[pl.BlockSpec((None, Cb, W), lambda b, c: (b, c, 0)),
                   pl.BlockSpec((None, W, W), lambda b, c: (b, 0, 0))],
        out_shape=[jax.ShapeDtypeStruct((B, L, W), BF16), jax.ShapeDtypeStruct((B, W, W), F32)],
        scratch_shapes=[pltpu.VMEM((W, W), F32)],
        compiler_params=_params("parallel", "arbitrary"),
        name="hgrn",
    )(a, lb, gain, s0)


def _log_sigmoid(x):
    return jnp.minimum(x, 0.0) - jnp.log1p(jnp.exp(-jnp.abs(x)))


def _mlstm_kernel(C, Cb, l_real, cg_ref, gt_ref, cw_ref, cb_ref, bf_ref, gain_ref, c0_ref, n0_ref, m0_ref,
                  tail0_ref, o_ref, cout_ref, nout_ref, mout_ref, ct_ref, n_ref, m_ref, tail_ref):
    c = pl.program_id(1)

    @pl.when(c == 0)
    def _():
        ct_ref[...] = c0_ref[...]
        n_ref[...] = n0_ref[...]
        m_ref[...] = m0_ref[...]
        tail_ref[...] = tail0_ref[...]

    W = REC_W
    cg = _pad_rows(cg_ref[...], C)
    gt = _pad_rows(gt_ref[...], C)
    valid = (c * C + lax.broadcasted_iota(jnp.int32, (C, 1), 0)) < l_real

    qk_pre = cg[:, 0:2 * W]
    xp = jnp.concatenate([tail_ref[...], qk_pre], axis=0)
    conv = cb_ref[...]
    base = SUB - (CONV_TAPS - 1)
    for j in range(CONV_TAPS):
        conv = conv + cw_ref[j:j + 1, :] * xp[base + j:base + j + C, :]
    tail_ref[...] = qk_pre[C - SUB:C, :]
    qk = _silu(conv)
    qc = qk[:, 0:W]
    kc = qk[:, W:2 * W] * (HEAD_W ** -0.5)
    vc = cg[:, 2 * W:3 * W]
    oc = cg[:, 3 * W:4 * W]

    lf = jnp.where(valid, _log_sigmoid(gt + bf_ref[...]), 0.0)
    li = jnp.where(valid, gt, NEG)
    fcum = _dot(_tril_ones(C), lf, HIGHEST)
    lane_r = lax.broadcasted_iota(jnp.int32, (ROUTE_LANES, W), 0)
    lane_c = lax.broadcasted_iota(jnp.int32, (ROUTE_LANES, W), 1) // HEAD_W
    li_b = _dot(li, (lane_r == lane_c).astype(F32), HIGHEST)
    f_b = _dot(fcum, (lane_r == lane_c + N_HEADS).astype(F32), HIGHEST)
    f_t = fcum.T
    li_t = li.T

    hm = _head_lane_masks(W)
    bd = _block_diag_mask(W, HEAD_W)
    causal = lax.broadcasted_iota(jnp.int32, (C, C), 0) >= lax.broadcasted_iota(jnp.int32, (C, C), 1)
    m_prev = m_ref[...]
    kc_bf = kc.astype(BF16)
    num = jnp.zeros((C, W), F32)
    den = jnp.zeros((C, W), F32)
    cs_b = jnp.zeros((C, W), F32)
    mt_b = jnp.zeros((C, W), F32)
    for h in range(N_HEADS):
        f_col = fcum[:, N_HEADS + h:N_HEADS + h + 1]
        log_d = jnp.where(causal, f_col - f_t[N_HEADS + h:N_HEADS + h + 1, :] + li_t[h:h + 1, :], NEG)
        log_s = f_col + m_prev[:, h * HEAD_W:h * HEAD_W + 1]
        m_t = jnp.maximum(jnp.max(log_d, axis=1, keepdims=True), log_s)
        w = _nt((qc * hm[h]).astype(BF16), kc_bf) * jnp.exp(log_d - m_t)
        num = num + _dot(w.astype(BF16), (vc * hm[h]).astype(BF16))
        den = den + jnp.sum(w, axis=1, keepdims=True) * hm[h]
        cs_b = cs_b + jnp.exp(log_s - m_t) * hm[h]
        mt_b = mt_b + m_t * hm[h]

    ct = ct_ref[...]
    n_row = n_ref[...]
    num = num + cs_b * _nt(qc.astype(BF16), ct.astype(BF16))
    den = den + cs_b * _dot((qc * n_row).astype(BF16), bd.astype(BF16))
    hh = num / jnp.maximum(jnp.abs(den), jnp.exp(-mt_b))

    m_new = mt_b[C - 1:C, :]
    f_end = f_b[C - 1:C, :]
    ws = jnp.exp(f_end - f_b + li_b - m_new)
    decay = jnp.exp(f_end + m_prev - m_new)
    ct_new = ct * decay + jnp.where(bd, _tn((vc * ws).astype(BF16), kc_bf), 0.0)
    n_new = decay * n_row + jnp.sum(ws * kc, axis=0, keepdims=True)
    ct_ref[...] = ct_new
    n_ref[...] = n_new
    m_ref[...] = m_new
    cout_ref[...] = ct_new
    nout_ref[...] = n_new
    mout_ref[...] = m_new

    ms = _dot(hh * hh, bd.astype(F32), HIGHEST) * (1.0 / HEAD_W)
    y = _sigmoid(oc) * (hh * lax.rsqrt(ms + NORM_EPS) * gain_ref[...])
    o_ref[...] = y[:Cb].astype(o_ref.dtype)


def _mlstm(cg, gates, conv_w, conv_b, bias_f, gain, c0, n0, m0, tail0, l_real):
    B, L, _ = cg.shape
    Cb = min(REC_CHUNK, L)
    W = REC_W
    const = lambda shape: pl.BlockSpec(shape, lambda b, c: (0,) * len(shape))
    per_b = lambda shape: pl.BlockSpec((None,) + shape, lambda b, c: (b,) + (0,) * len(shape))
    return pl.pallas_call(
        functools.partial(_mlstm_kernel, REC_CHUNK, Cb, l_real),
        grid=(B, L // Cb),
        in_specs=[pl.BlockSpec((None, Cb, 4 * W), lambda b, c: (b, c, 0)),
                  pl.BlockSpec((None, Cb, ROUTE_LANES), lambda b, c: (b, c, 0)),
                  const((CONV_TAPS, 2 * W)), const((1, 2 * W)), const((1, ROUTE_LANES)), const((1, W)),
                  per_b((W, W)), per_b((1, W)), per_b((1, W)), per_b((SUB, 2 * W))],
        out_specs=[pl.BlockSpec((None, Cb, W), lambda b, c: (b, c, 0)),
                   per_b((W, W)), per_b((1, W)), per_b((1, W))],
        out_shape=[jax.ShapeDtypeStruct((B, L, W), BF16), jax.ShapeDtypeStruct((B, W, W), F32),
                   jax.ShapeDtypeStruct((B, 1, W), F32), jax.ShapeDtypeStruct((B, 1, W), F32)],
        scratch_shapes=[pltpu.VMEM((W, W), F32), pltpu.VMEM((1, W), F32), pltpu.VMEM((1, W), F32),
                        pltpu.VMEM((SUB, 2 * W), F32)],
        compiler_params=_params("parallel", "arbitrary"),
        name="mlstm",
    )(cg, gates, conv_w, conv_b, bias_f, gain, c0, n0, m0, tail0)


def _diff_finish(o1, o2, lam, gain, out_scale):
    o = o1 - lam * o2
    ms = jnp.mean(o * o, axis=-1, keepdims=True)
    return out_scale * (o * lax.rsqrt(ms + NORM_EPS) * gain)


def _attn_kernel(T, out_scale, qi_ref, ki_ref, q_ref, k_ref, v_ref, lam_ref, gain_ref, o_ref,
                 qq_sc, m_sc, l_sc, acc_sc):
    step_id = pl.program_id(1)
    qi = qi_ref[step_id]
    ki = ki_ref[step_id]
    heads = [slice(h * ATT_W, (h + 1) * ATT_W) for h in range(N_HEADS)]
    rb = ATT_ROWS if (2 * T) % ATT_ROWS == 0 else T
    lane_tiles = T // ATT_W

    @pl.when(ki == 0)
    def _():
        lane = lax.broadcasted_iota(jnp.int32, (1, ATT_W), 1)
        for h in range(N_HEADS):
            q = q_ref[:, heads[h]]
            zero = jnp.zeros_like(q)
            qq_sc[h, 0:T, :] = jnp.where(lane < HEAD_W, q, zero)
            qq_sc[h, T:2 * T, :] = jnp.where(lane >= HEAD_W, q, zero)
        m_sc[...] = jnp.full(m_sc.shape, NEG, F32)
        l_sc[...] = jnp.zeros(l_sc.shape, F32)
        acc_sc[...] = jnp.zeros(acc_sc.shape, F32)

    def step(diagonal):
        for h in range(N_HEADS):
            k_h = k_ref[:, heads[h]]
            v_h = v_ref[:, heads[h]]
            for r0 in range(0, 2 * T, rb):
                rows = slice(r0, r0 + rb)
                s = _nt(qq_sc[h, rows, :], k_h)
                if diagonal:
                    row = (r0 + lax.broadcasted_iota(jnp.int32, (rb, 1), 0)) % T
                    col = lax.broadcasted_iota(jnp.int32, (1, T), 1)
                    s = jnp.where(row >= col, s, NEG)
                m_old = m_sc[h, rows, :]
                m_new = jnp.maximum(m_old, jnp.max(s, axis=1, keepdims=True))
                alpha = jnp.exp(m_old - m_new)
                p = jnp.exp(s - jnp.concatenate([m_new] * lane_tiles, axis=1))
                p_lanes = p[:, 0:ATT_W]
                for t in range(1, lane_tiles):
                    p_lanes = p_lanes + p[:, t * ATT_W:(t + 1) * ATT_W]
                l_sc[h, rows, :] = alpha * l_sc[h, rows, :] + p_lanes
                acc_sc[h, rows, :] = alpha * acc_sc[h, rows, :] + _dot(p.astype(BF16), v_h)
                m_sc[h, rows, :] = m_new

    @pl.when(ki < qi)
    def _():
        step(False)

    @pl.when(ki == qi)
    def _():
        step(True)
        for h in range(N_HEADS):
            on = acc_sc[h] / jnp.sum(l_sc[h], axis=1, keepdims=True)
            y = _diff_finish(on[:T], on[T:], lam_ref[...], gain_ref[:, heads[h]], out_scale)
            o_ref[:, heads[h]] = y.astype(o_ref.dtype)


def _attn_prompt(q, k, v, lam_row, gain, out_scale):
    B, L, _ = q.shape
    T = ATT_TILE if L % ATT_TILE == 0 else SEQ_TILE
    n = L // T
    pairs = [(a, b) for a in range(n) for b in range(a + 1)]
    qi_arr = jnp.asarray([a for a, _ in pairs], jnp.int32)
    ki_arr = jnp.asarray([b for _, b in pairs], jnp.int32)
    q_spec = pl.BlockSpec((None, T, ATT_ALL), lambda b, p, qi, ki: (b, qi[p], 0))
    kv_spec = pl.BlockSpec((None, T, ATT_ALL), lambda b, p, qi, ki: (b, ki[p], 0))
    grid_spec = pltpu.PrefetchScalarGridSpec(
        num_scalar_prefetch=2,
        grid=(B, len(pairs)),
        in_specs=[q_spec, kv_spec, kv_spec,
                  pl.BlockSpec((1, ATT_W), lambda b, p, qi, ki: (0, 0)),
                  pl.BlockSpec((1, ATT_ALL), lambda b, p, qi, ki: (0, 0))],
        out_specs=q_spec,
        scratch_shapes=[pltpu.VMEM((N_HEADS, 2 * T, ATT_W), BF16), pltpu.VMEM((N_HEADS, 2 * T, ATT_W), F32),
                        pltpu.VMEM((N_HEADS, 2 * T, ATT_W), F32), pltpu.VMEM((N_HEADS, 2 * T, ATT_W), F32)],
    )
    return pl.pallas_call(
        functools.partial(_attn_kernel, T, out_scale),
        grid_spec=grid_spec,
        out_shape=jax.ShapeDtypeStruct((B, L, ATT_ALL), BF16),
        compiler_params=_params("parallel", "arbitrary"),
        name="attn_prompt",
    )(qi_arr, ki_arr, q, k, v, lam_row, gain)


def _dec_attn_kernel(npg, ls, page, out_scale, pt_ref, q_ref, kn_ref, vn_ref, lam_ref, gain_ref, *refs):
    del pt_ref
    k_refs, v_refs = refs[:npg], refs[npg:2 * npg]
    o_ref, qs_sc, m_sc, l_sc, acc_sc = refs[2 * npg:]
    j = pl.program_id(1)
    rows = 2 * N_HEADS * ls
    cols = page * N_HEADS
    row_head = lax.broadcasted_iota(jnp.int32, (rows, 1), 0) // (2 * ls)

    @pl.when(j == 0)
    def _():
        q = q_ref[...].astype(F32)
        lane = lax.broadcasted_iota(jnp.int32, (1, ATT_W), 1)
        parts = []
        for h in range(N_HEADS):
            qh = q[:, h * ATT_W:(h + 1) * ATT_W]
            parts += [jnp.where(lane < HEAD_W, qh, 0.0), jnp.where(lane >= HEAD_W, qh, 0.0)]
        qs_sc[...] = jnp.concatenate(parts, axis=0).astype(BF16)
        m_sc[...] = jnp.full(m_sc.shape, NEG, F32)
        l_sc[...] = jnp.zeros(l_sc.shape, F32)
        acc_sc[...] = jnp.zeros(acc_sc.shape, F32)

    qs = qs_sc[...]

    def update(scores, values):
        m_old = m_sc[...]
        m_new = m_old
        for s in scores:
            m_new = jnp.maximum(m_new, jnp.max(s, axis=1, keepdims=True))
        alpha = jnp.exp(m_old - m_new)
        l_new = alpha * l_sc[...]
        acc = alpha * acc_sc[...]
        for s, val in zip(scores, values):
            p = jnp.exp(s - m_new)
            l_new = l_new + jnp.sum(p, axis=1, keepdims=True)
            acc = acc + _dot(p.astype(BF16), val)
        l_sc[...] = l_new
        acc_sc[...] = acc
        m_sc[...] = m_new

    own_head = (lax.broadcasted_iota(jnp.int32, (1, cols), 1) % N_HEADS) == row_head
    update([jnp.where(own_head, _nt(qs, kr[...].astype(BF16)), NEG) for kr in k_refs],
           [vr[...].astype(BF16) for vr in v_refs])

    @pl.when(j == pl.num_programs(1) - 1)
    def _():
        n_new = ls * N_HEADS
        kn = _pad_rows(kn_ref[...], ATT_W).astype(BF16)
        vn = _pad_rows(vn_ref[...], ATT_W).astype(BF16)
        col = lax.broadcasted_iota(jnp.int32, (1, ATT_W), 1)
        q_idx = lax.broadcasted_iota(jnp.int32, (rows, 1), 0) % ls
        ok = (col % N_HEADS == row_head) & (col // N_HEADS <= q_idx) & (col < n_new)
        update([jnp.where(ok, _nt(qs, kn), NEG)], [vn])
        on = acc_sc[...] / l_sc[...]
        gain = gain_ref[...]
        parts = []
        for h in range(N_HEADS):
            r0 = h * 2 * ls
            parts.append(_diff_finish(on[r0:r0 + ls], on[r0 + ls:r0 + 2 * ls], lam_ref[...],
                                      gain[:, h * ATT_W:(h + 1) * ATT_W], out_scale))
        o_ref[...] = jnp.concatenate(parts, axis=1).astype(o_ref.dtype)


def _attn_sample(q, k_new, v_new, cache_k, cache_v, page_table, layer, lam_row, gain, out_scale):
    B, ls, _ = q.shape
    n_pages = page_table.shape[1]
    page = cache_k.shape[2]
    cols = page * N_HEADS
    npg = PAGES_PER_STEP if n_pages % PAGES_PER_STEP == 0 else 1
    ck = cache_k.reshape(cache_k.shape[0], cache_k.shape[1], cols, ATT_W)
    cv = cache_v.reshape(cache_v.shape[0], cache_v.shape[1], cols, ATT_W)
    kn = k_new.reshape(B, ls * N_HEADS, ATT_W)
    vn = v_new.reshape(B, ls * N_HEADS, ATT_W)

    def page_spec(i):
        return pl.BlockSpec((None, None, cols, ATT_W), lambda b, j, pt: (layer, pt[b, j * npg + i], 0, 0))

    per_b = pl.BlockSpec((None, ls, ATT_ALL), lambda b, j, pt: (b, 0, 0))
    new_spec = pl.BlockSpec((None, ls * N_HEADS, ATT_W), lambda b, j, pt: (b, 0, 0))
    rows = 2 * N_HEADS * ls
    grid_spec = pltpu.PrefetchScalarGridSpec(
        num_scalar_prefetch=1,
        grid=(B, n_pages // npg),
        in_specs=[per_b, new_spec, new_spec,
                  pl.BlockSpec((1, ATT_W), lambda b, j, pt: (0, 0)),
                  pl.BlockSpec((1, ATT_ALL), lambda b, j, pt: (0, 0))]
                 + [page_spec(i) for i in range(npg)] * 2,
        out_specs=per_b,
        scratch_shapes=[pltpu.VMEM((rows, ATT_W), BF16), pltpu.VMEM((rows, 1), F32), pltpu.VMEM((rows, 1), F32),
                        pltpu.VMEM((rows, ATT_W), F32)],
    )
    return pl.pallas_call(
        functools.partial(_dec_attn_kernel, npg, ls, page, out_scale),
        grid_spec=grid_spec,
        out_shape=jax.ShapeDtypeStruct((B, ls, ATT_ALL), BF16),
        compiler_params=_params("parallel", "arbitrary"),
        name="attn_sample",
    )(page_table, q, kn, vn, lam_row, gain, *([ck] * npg), *([cv] * npg))


def _outproj_kernel(alpha, ma_ref, mb_ref, mc_ref, h_ref, wo_ref, g_ref, b_ref, wr_ref, br_ref,
                    h1_ref, h1t_ref, eid_ref, gate_ref):
    W = REC_W
    mix = (_dot(ma_ref[...], wo_ref[0:W, :]) + _dot(mb_ref[...], wo_ref[W:W + ATT_ALL, :])
           + _dot(mc_ref[...], wo_ref[W + ATT_ALL:2 * W + ATT_ALL, :]))
    h1 = _layernorm(alpha * h_ref[...] + mix, g_ref[...], b_ref[...])
    h1_ref[...] = h1
    _to_row_tiles(h1t_ref, h1)

    logits = _dot(h1.astype(BF16), wr_ref[...]) + br_ref[...]
    lane = lax.broadcasted_iota(jnp.int32, (1, ROUTE_LANES), 1)
    lane_f = lane.astype(F32)
    far = float(ROUTE_LANES)

    gl = jnp.where(lane < N_GROUPS, logits, NEG)
    g_max = jnp.max(gl, axis=1, keepdims=True)
    g_top = 1.0 / jnp.sum(jnp.exp(gl - g_max), axis=1, keepdims=True)
    g_idx = jnp.min(jnp.where(gl == g_max, lane_f, far), axis=1, keepdims=True)

    e_lane = lane_f - float(EXPERT_LANE0)
    in_group = (e_lane >= g_idx * GROUP_EXPERTS) & (e_lane < (g_idx + 1.0) * GROUP_EXPERTS)
    el = jnp.where(in_group, logits, NEG)
    e_max = jnp.max(el, axis=1, keepdims=True)
    e_sum = jnp.sum(jnp.exp(el - e_max), axis=1, keepdims=True)
    idx1 = jnp.min(jnp.where(el == e_max, lane_f, far), axis=1, keepdims=True)
    el2 = jnp.where(lane_f == idx1, NEG, el)
    e_max2 = jnp.max(el2, axis=1, keepdims=True)
    idx2 = jnp.min(jnp.where(el2 == e_max2, lane_f, far), axis=1, keepdims=True)
    p1 = 1.0 / e_sum
    p2 = jnp.exp(e_max2 - e_max) / e_sum
    tot = p1 + p2
    gate_ref[...] = jnp.where(lane == 0, g_top * (p1 / tot), jnp.where(lane == 1, g_top * (p2 / tot), 0.0))
    eid = jnp.where(lane == 0, idx1 - float(EXPERT_LANE0), jnp.where(lane == 1, idx2 - float(EXPERT_LANE0), 0.0))
    eid_ref[...] = eid.astype(jnp.int32)


def _outproj(ma, mb, mc, h, wo, g, b, wr, br, alpha):
    T, D = h.shape
    tm = min(ROW_TILE, T)
    tile = lambda n: pl.BlockSpec((tm, n), lambda i: (i, 0))
    const = lambda r, n: pl.BlockSpec((r, n), lambda i: (0, 0))
    return pl.pallas_call(
        functools.partial(_outproj_kernel, alpha),
        grid=(T // tm,),
        in_specs=[tile(REC_W), tile(ATT_ALL), tile(REC_W), tile(D), const(2 * REC_W + ATT_ALL, D),
                  const(1, D), const(1, D), const(D, ROUTE_LANES), const(1, ROUTE_LANES)],
        out_specs=[tile(D), pl.BlockSpec((tm, D // LANES, LANES), lambda i: (i, 0, 0)), tile(ROUTE_LANES),
                   tile(ROUTE_LANES)],
        out_shape=[jax.ShapeDtypeStruct((T, D), F32), jax.ShapeDtypeStruct((T, D // LANES, LANES), F32),
                   jax.ShapeDtypeStruct((T, ROUTE_LANES), jnp.int32),
                   jax.ShapeDtypeStruct((T, ROUTE_LANES), F32)],
        compiler_params=_params("parallel"),
        name="outproj",
    )(ma, mb, mc, h, wo, g, b, wr, br)


def _expert_kernel(row_ref, blke_ref, nval_ref, x_hbm, wg_ref, wu_ref, wd_ref, out_hbm,
                   xbuf, ybuf, gsem, ssem):
    del blke_ref
    i = pl.program_id(0)
    nblk = pl.num_programs(0)
    cur = i % 2

    def start_gather(blk, buf):
        @pl.when(nval_ref[blk] > 0)
        def _():
            def body(r, carry):
                tok = lax.shift_right_logical(row_ref[blk * MOE_ROWS + r], 1)
                pltpu.make_async_copy(x_hbm.at[pl.ds(tok, 1)], xbuf.at[buf, pl.ds(r, 1)], gsem.at[buf]).start()
                return carry
            lax.fori_loop(0, MOE_ROWS, body, 0, unroll=SUB)

    def start_scatter(blk, buf):
        def one(r):
            code = row_ref[blk * MOE_ROWS + r]
            pltpu.make_async_copy(ybuf.at[buf, pl.ds(r, 1)],
                                  out_hbm.at[code & 1, pl.ds(lax.shift_right_logical(code, 1), 1)],
                                  ssem.at[buf]).start()

        n = nval_ref[blk]
        full = lax.shift_right_logical(n, 3)

        def group(g, carry):
            for u in range(SUB):
                one(g * SUB + u)
            return carry
        lax.fori_loop(0, full, group, 0)

        def rest(r, carry):
            one(r)
            return carry
        lax.fori_loop(full * SUB, n, rest, 0)

    def wait_scatter(blk, buf):
        n = nval_ref[blk]

        @pl.when(n > 0)
        def _():
            pltpu.make_async_copy(ybuf.at[buf, pl.ds(0, n)], out_hbm.at[0, pl.ds(0, n)], ssem.at[buf]).wait()

    @pl.when(i == 0)
    def _():
        start_gather(0, 0)

    @pl.when(i + 1 < nblk)
    def _():
        start_gather(i + 1, 1 - cur)

    @pl.when(i >= 2)
    def _():
        wait_scatter(i - 2, cur)

    @pl.when(nval_ref[i] > 0)
    def _():
        pltpu.make_async_copy(x_hbm.at[pl.ds(0, MOE_ROWS)], xbuf.at[cur], gsem.at[cur]).wait()
        xb = _from_row_tiles(xbuf.at[cur]).astype(BF16)
        hid = _silu(_dot(xb, wg_ref[...])) * _dot(xb, wu_ref[...])
        _to_row_tiles(ybuf.at[cur], _dot(hid.astype(BF16), wd_ref[...]))
        start_scatter(i, cur)

    @pl.when(i == nblk - 1)
    def _():
        @pl.when(i >= 1)
        def _():
            wait_scatter(i - 1, 1 - cur)
        wait_scatter(i, cur)


def _experts(x_tiles, row_code, blk_e, blk_nval, wg, wu, wd):
    T, chunks, _ = x_tiles.shape
    D = chunks * LANES
    n_blocks = blk_e.shape[0]
    de = wg.shape[2]
    grid_spec = pltpu.PrefetchScalarGridSpec(
        num_scalar_prefetch=3,
        grid=(n_blocks,),
        in_specs=[pl.BlockSpec(memory_space=pl.ANY),
                  pl.BlockSpec((None, D, de), lambda i, code, be, nv: (be[i], 0, 0)),
                  pl.BlockSpec((None, D, de), lambda i, code, be, nv: (be[i], 0, 0)),
                  pl.BlockSpec((None, de, D), lambda i, code, be, nv: (be[i], 0, 0))],
        out_specs=pl.BlockSpec(memory_space=pl.ANY),
        scratch_shapes=[pltpu.VMEM((2, MOE_ROWS, chunks, LANES), F32), pltpu.VMEM((2, MOE_ROWS, chunks, LANES), F32),
                        pltpu.SemaphoreType.DMA((2,)), pltpu.SemaphoreType.DMA((2,))],
    )
    return pl.pallas_call(
        _expert_kernel,
        grid_spec=grid_spec,
        out_shape=jax.ShapeDtypeStruct((2, T, chunks, LANES), F32),
        compiler_params=_params("arbitrary"),
        name="experts",
    )(row_code, blk_e, blk_nval, x_tiles, wg, wu, wd)


def _dispatch(eid, T):
    K = 2
    A = T * K
    flat_e = eid[:, :K].reshape(-1)
    onehot = (flat_e[:, None] == jnp.arange(N_EXPERTS, dtype=jnp.int32)[None, :]).astype(F32)
    a_pad = -(-A // MOE_ROWS) * MOE_ROWS
    oh3 = jnp.pad(onehot, ((0, a_pad - A), (0, 0))).reshape(a_pad // MOE_ROWS, MOE_ROWS, N_EXPERTS)
    tril = jnp.tril(jnp.ones((MOE_ROWS, MOE_ROWS), F32))
    within = jnp.einsum('ts,bse->bte', tril, oh3, precision=HIGHEST)
    blk_tot = within[:, -1, :]
    before = jnp.cumsum(blk_tot, axis=0) - blk_tot
    rank = jnp.sum((within + before[:, None, :]) * oh3, axis=-1).reshape(a_pad)[:A] - 1.0
    counts = jnp.sum(blk_tot, axis=0).astype(jnp.int32)
    padded = (counts + MOE_ROWS - 1) // MOE_ROWS * MOE_ROWS
    pad_end = jnp.cumsum(padded)
    pad_start = pad_end - padded
    dest = jnp.sum(onehot * pad_start.astype(F32)[None, :], axis=-1) + rank
    n_blocks = -(-A // MOE_ROWS) + N_EXPERTS
    R = n_blocks * MOE_ROWS
    row_code = jnp.zeros((R,), jnp.int32).at[dest.astype(jnp.int32)].set(jnp.arange(A, dtype=jnp.int32))
    blk_start = jnp.arange(n_blocks, dtype=jnp.int32) * MOE_ROWS
    blk_e = jnp.minimum(jnp.sum((pad_end[None, :] <= blk_start[:, None]).astype(jnp.int32), axis=1), N_EXPERTS - 1)
    blk_nval = jnp.clip((pad_start + counts)[blk_e] - blk_start, 0, MOE_ROWS).astype(jnp.int32)
    return row_code, blk_e.astype(jnp.int32), blk_nval


def _combine_kernel(alpha, h_ref, y_ref, gate_ref, g_ref, b_ref, o_ref):
    gate = gate_ref[...]
    y = gate[:, 0:1] * _from_row_tiles(y_ref.at[0]) + gate[:, 1:2] * _from_row_tiles(y_ref.at[1])
    o_ref[...] = _layernorm(alpha * h_ref[...] + y, g_ref[...], b_ref[...])


def _combine(h1, slots, gates, g, b, alpha):
    T, D = h1.shape
    tm = min(ROW_TILE, T)
    return pl.pallas_call(
        functools.partial(_combine_kernel, alpha),
        grid=(T // tm,),
        in_specs=[pl.BlockSpec((tm, D), lambda i: (i, 0)), pl.BlockSpec((2, tm, D // LANES, LANES), lambda i: (0, i, 0, 0)),
                  pl.BlockSpec((tm, ROUTE_LANES), lambda i: (i, 0)),
                  pl.BlockSpec((1, D), lambda i: (0, 0)), pl.BlockSpec((1, D), lambda i: (0, 0))],
        out_specs=pl.BlockSpec((tm, D), lambda i: (i, 0)),
        out_shape=jax.ShapeDtypeStruct((T, D), F32),
        compiler_params=_params("parallel"),
        name="combine",
    )(h1, slots, gates, g, b)


def _block_diag(blocks):
    B, H, n, _ = blocks.shape
    eye = jnp.eye(H, dtype=blocks.dtype)
    return jnp.einsum('bhij,hg->bhigj', blocks, eye).reshape(B, H * n, H * n)


def _diag_blocks(mat, H):
    B, n = mat.shape[0], mat.shape[1] // H
    m5 = mat.reshape(B, H, n, H, n)
    return jnp.stack([m5[:, h, :, h, :] for h in range(H)], axis=1)


def _layer(x, do_ln, B, L, l_real, lw, state, past):
    outs = _inproj(x, lw['ln_g'], lw['ln_b'], lw['w_in'], lw['b_in'], do_ln)
    a, qb, kb, vb, cg, gates, kb16, vb16 = outs[:8]
    h = outs[8] if do_ln else x
    s3 = lambda z: z.reshape(B, L, z.shape[-1])

    mix_a, st_a = _hgrn(s3(a), lw['lb'], lw['norm_a'], state['hgrn'], l_real)
    mix_c, ct, n_c, m_c = _mlstm(s3(cg), s3(gates), lw['conv_w'], lw['conv_b'], lw['b_f'], lw['norm_c'],
                                 state['c'], state['n'], state['m'], state['tail'], l_real)
    if past is None:
        mix_b = _attn_prompt(s3(qb), s3(kb16), s3(vb16), lw['lam'], lw['norm_b'], lw['out_scale'])
    else:
        mix_b = _attn_sample(s3(qb), kb, vb, past[0], past[1], past[2], lw['layer'], lw['lam'],
                             lw['norm_b'], lw['out_scale'])

    flat = lambda z: z.reshape(B * L, z.shape[-1])
    h1, h1_tiles, eid, gate = _outproj(flat(mix_a), flat(mix_b), flat(mix_c), h, lw['w_out'], lw['ln1_g'], lw['ln1_b'],
                             lw['w_route'], lw['b_route'], lw['alpha'])
    row_code, blk_e, blk_nval = _dispatch(eid, B * L)
    slots = _experts(h1_tiles, row_code, blk_e, blk_nval, lw['w_eg'], lw['w_eu'], lw['w_ed'])
    h2 = _combine(h1, slots, gate, lw['ln2_g'], lw['ln2_b'], lw['alpha'])

    new_state = {
        'k': kb.reshape(B, L, N_HEADS, ATT_W), 'v': vb.reshape(B, L, N_HEADS, ATT_W),
        'hgrn': jnp.swapaxes(_diag_blocks(st_a, N_HEADS), -1, -2),
        'c': _diag_blocks(ct, N_HEADS),
        'n': n_c.reshape(B, N_HEADS, HEAD_W),
        'm': m_c.reshape(B, N_HEADS, HEAD_W)[:, :, 0],
        'conv': s3(cg)[:, l_real - (CONV_TAPS - 1):l_real, :2 * REC_W],
    }
    return h2, new_state


def kernel(x_prompt, x_sample, cache_k, cache_v, page_table, state_hgrn, state_mlstm_c, state_mlstm_n, state_mlstm_m, state_mlstm_conv, meta_tokens, ln_emb_g, ln_emb_b, w_in, b_in, b_mlstm_f, conv_w, conv_b, hgrn_lb_logits, lambda_q1, lambda_k1, lambda_q2, lambda_k2, norm_a, norm_b, norm_c, w_out, ln1_g, ln1_b, w_router_group, b_router_group, w_router_expert, b_router_expert, w_exp_gate, w_exp_up, w_exp_down, ln2_g, ln2_b):
    depth = w_in.shape[0]
    Bp, Lq, D = x_prompt.shape
    Bs, Ls, _ = x_sample.shape
    n_meta = meta_tokens.shape[0]
    l_real = n_meta + Lq
    Lp = -(-l_real // SEQ_TILE) * SEQ_TILE
    alpha = (2.0 * depth) ** 0.25
    row = lambda z: z.reshape(1, -1).astype(F32)

    meta = jnp.broadcast_to(meta_tokens[None].astype(x_prompt.dtype), (Bp, n_meta, D))
    xp = jnp.concatenate([meta, x_prompt, jnp.zeros((Bp, Lp - l_real, D), x_prompt.dtype)], axis=1)
    hp = xp.reshape(Bp * Lp, D)
    hs = x_sample.reshape(Bs * Ls, D)

    lb_sm = jax.nn.softmax(hgrn_lb_logits.astype(F32), axis=0)
    lb_all = jnp.cumsum(lb_sm, axis=0) - lb_sm[0]

    zeros_p = {
        'hgrn': jnp.zeros((Bp, REC_W, REC_W), F32), 'c': jnp.zeros((Bp, REC_W, REC_W), F32),
        'n': jnp.zeros((Bp, 1, REC_W), F32), 'm': jnp.zeros((Bp, 1, REC_W), F32),
        'tail': jnp.zeros((Bp, SUB, 2 * REC_W), F32),
    }

    res_p, res_s = [], []
    for l in range(depth):
        lam_init = 0.8 - 0.6 * math.exp(-0.3 * l)
        lam = (jnp.exp(jnp.sum(lambda_q1[l].astype(F32) * lambda_k1[l].astype(F32)))
               - jnp.exp(jnp.sum(lambda_q2[l].astype(F32) * lambda_k2[l].astype(F32))) + lam_init)
        pad_cols = N_PROJ_PAD - N_PROJ_RAW
        bias_f = jnp.zeros((ROUTE_LANES,), F32).at[N_HEADS:2 * N_HEADS].set(b_mlstm_f[l].astype(F32))
        w_route = jnp.concatenate([w_router_group[l], w_router_expert[l]], axis=1)
        b_route = jnp.concatenate([b_router_group[l], b_router_expert[l]])
        r_pad = ROUTE_LANES - w_route.shape[1]
        lw = {
            'layer': l, 'alpha': alpha, 'out_scale': 1.0 - lam_init,
            'ln_g': row(ln_emb_g), 'ln_b': row(ln_emb_b),
            'w_in': jnp.pad(w_in[l], ((0, 0), (0, pad_cols))).astype(BF16),
            'b_in': row(jnp.pad(b_in[l], (0, pad_cols))),
            'lb': row(lb_all[l]), 'norm_a': row(norm_a[l]), 'norm_b': row(norm_b[l]), 'norm_c': row(norm_c[l]),
            'lam': jnp.full((1, ATT_W), lam, F32),
            'conv_w': conv_w[l].astype(F32), 'conv_b': row(conv_b[l]), 'b_f': row(bias_f),
            'w_out': w_out[l].astype(BF16), 'ln1_g': row(ln1_g[l]), 'ln1_b': row(ln1_b[l]),
            'w_route': jnp.pad(w_route, ((0, 0), (0, r_pad))).astype(BF16), 'b_route': row(jnp.pad(b_route, (0, r_pad))),
            'w_eg': w_exp_gate[l].astype(BF16), 'w_eu': w_exp_up[l].astype(BF16), 'w_ed': w_exp_down[l].astype(BF16),
            'ln2_g': row(ln2_g[l]), 'ln2_b': row(ln2_b[l]),
        }
        hp, st_p = _layer(hp, l == 0, Bp, Lp, l_real, lw, zeros_p, None)

        state_s = {
            'hgrn': _block_diag(jnp.swapaxes(state_hgrn[l].astype(F32), -1, -2)),
            'c': _block_diag(state_mlstm_c[l].astype(F32)),
            'n': state_mlstm_n[l].astype(F32).reshape(Bs, 1, REC_W),
            'm': jnp.repeat(state_mlstm_m[l].astype(F32), HEAD_W, axis=-1).reshape(Bs, 1, REC_W),
            'tail': jnp.pad(state_mlstm_conv[l].astype(F32), ((0, 0), (SUB - (CONV_TAPS - 1), 0), (0, 0))),
        }
        hs, st_s = _layer(hs, l == 0, Bs, Ls, Ls, lw, state_s, (cache_k, cache_v, page_table))
        res_p.append(st_p)
        res_s.append(st_s)

    y_prompt = hp.reshape(Bp, Lp, D)[:, n_meta:l_real]
    y_sample = hs.reshape(Bs, Ls, D)
    stack = lambda res, key, f=lambda z: z: jnp.stack([f(r[key]) for r in res])
    kv_p = lambda z: z[:, :l_real]
    kv_s = lambda z: z
    return (y_prompt, y_sample,
            stack(res_p, 'k', kv_p), stack(res_p, 'v', kv_p), stack(res_s, 'k', kv_s), stack(res_s, 'v', kv_s),
            stack(res_p, 'hgrn'), stack(res_s, 'hgrn'),
            stack(res_p, 'c'), stack(res_s, 'c'),
            stack(res_p, 'n'), stack(res_s, 'n'),
            stack(res_p, 'm'), stack(res_s, 'm'),
            stack(res_p, 'conv'), stack(res_s, 'conv'))
```

```python
import functools
import math

import jax
import jax.numpy as jnp
from jax import lax
from jax.experimental import pallas as pl
from jax.experimental.pallas import tpu as pltpu

F32 = jnp.float32
BF16 = jnp.bfloat16
HIGHEST = lax.Precision.HIGHEST

N_META = 16
N_HEADS = 4
HEAD_W = 64
ATT_W = 2 * HEAD_W
REC_W = N_HEADS * HEAD_W
ATT_ALL = N_HEADS * ATT_W
CONV_TAPS = 4
N_GROUPS = 4
GROUP_EXPERTS = 8
N_EXPERTS = N_GROUPS * GROUP_EXPERTS
LANES = 128
ROUTE_LANES = LANES
EXPERT_LANE0 = N_GROUPS
MOE_ROWS = 128
LN_EPS = 1e-5
NORM_EPS = 1e-6
NEG = -1e30
LOG2_E = 1.4426950408889634

SEQ_TILE = 128
REC_CHUNK = 128
SUB = 8
ATT_TILE = 384
ATT_ROWS = 768
PAGES_PER_STEP = 8
ROW_TILE = 256
INPROJ_TILES = (384, 256, 128)
VMEM_LIMIT = 48 * 1024 * 1024

IN_WIDTHS = (4 * REC_W, ATT_ALL, ATT_ALL, ATT_ALL, 4 * REC_W, ROUTE_LANES)
N_PROJ_RAW = 4 * REC_W + 3 * ATT_ALL + 4 * REC_W + 2 * N_HEADS
N_PROJ_PAD = sum(IN_WIDTHS)


def _nt(a, b):
    return lax.dot_general(a, b, (((1,), (1,)), ((), ())), preferred_element_type=F32)


def _tn(a, b):
    return lax.dot_general(a, b, (((0,), (0,)), ((), ())), preferred_element_type=F32)


def _dot(a, b, precision=None):
    return jnp.dot(a, b, preferred_element_type=F32, precision=precision)


def _layernorm(x, g, b):
    mu = jnp.mean(x, axis=-1, keepdims=True)
    xc = x - mu
    var = jnp.mean(xc * xc, axis=-1, keepdims=True)
    return xc * lax.rsqrt(var + LN_EPS) * g + b


def _sigmoid(x):
    return 1.0 / (1.0 + jnp.exp(-x))


def _silu(x):
    return x * _sigmoid(x)


def _head_lane_masks(width):
    lane = lax.broadcasted_iota(jnp.int32, (1, width), 1)
    per = width // N_HEADS
    return [(lane // per == h).astype(F32) for h in range(N_HEADS)]


def _block_diag_mask(n, blk):
    r = lax.broadcasted_iota(jnp.int32, (n, n), 0) // blk
    c = lax.broadcasted_iota(jnp.int32, (n, n), 1) // blk
    return r == c


def _prefix_sum_rows(x):
    n = x.shape[0]
    r = lax.broadcasted_iota(jnp.int32, (n, n), 0)
    c = lax.broadcasted_iota(jnp.int32, (n, n), 1)
    tril = (r >= c).astype(BF16)
    hi = x.astype(BF16)
    rest = x - hi.astype(F32)
    mid = rest.astype(BF16)
    lo = (rest - mid.astype(F32)).astype(BF16)
    return _dot(tril, hi) + (_dot(tril, mid) + _dot(tril, lo))


def _to_row_tiles(ref_at, x):
    for c in range(x.shape[1] // LANES):
        ref_at[:, c, :] = x[:, c * LANES:(c + 1) * LANES]


def _from_row_tiles(ref_at):
    return jnp.concatenate([ref_at[:, c, :] for c in range(ref_at.shape[1])], axis=1)


def _params(*sem):
    return pltpu.CompilerParams(dimension_semantics=sem, vmem_limit_bytes=VMEM_LIMIT)


def _inproj_kernel(do_ln, x_ref, g_ref, b_ref, w_ref, bias_ref, *outs):
    x = x_ref[...]
    if do_ln:
        x = _layernorm(x, g_ref[...], b_ref[...])
        outs[-1][...] = x
    xb = x.astype(BF16)
    col = 0
    for idx, width in enumerate(IN_WIDTHS):
        acc = _dot(xb, w_ref[:, col:col + width]) + bias_ref[:, col:col + width]
        if idx == 1:
            acc = acc * (HEAD_W ** -0.5 * LOG2_E)
        if idx in (2, 3):
            _to_row_tiles(outs[idx], acc)
            outs[len(IN_WIDTHS) + idx - 2][...] = acc.astype(BF16)
        else:
            outs[idx][...] = acc.astype(outs[idx].dtype)
        col += width


def _inproj(x, ln_g, ln_b, w, bias, do_ln, n_seq, l_out):
    T, D = x.shape
    L = T // n_seq
    tm = next((t for t in INPROJ_TILES if L % t == 0), L)
    nj = L // tm
    widths = IN_WIDTHS + (ATT_ALL, ATT_ALL)
    dtypes = (F32, BF16, F32, F32, F32, F32, BF16, BF16)
    out_shape = [jax.ShapeDtypeStruct((T, wd), dt) for wd, dt in zip(widths, dtypes)]
    out_specs = [pl.BlockSpec((tm, wd), lambda i: (i, 0)) for wd in widths]
    for idx in (2, 3):
        out_shape[idx] = jax.ShapeDtypeStruct((n_seq, l_out, N_HEADS, ATT_W), F32)
        out_specs[idx] = pl.BlockSpec((None, tm, N_HEADS, ATT_W), lambda i: (i // nj, i % nj, 0, 0))
    if do_ln:
        out_shape.append(jax.ShapeDtypeStruct((T, D), F32))
        out_specs.append(pl.BlockSpec((tm, D), lambda i: (i, 0)))
    row = lambda n: pl.BlockSpec((1, n), lambda i: (0, 0))
    return pl.pallas_call(
        functools.partial(_inproj_kernel, do_ln),
        grid=(T // tm,),
        in_specs=[pl.BlockSpec((tm, D), lambda i: (i, 0)), row(D), row(D),
                  pl.BlockSpec((D, N_PROJ_PAD), lambda i: (0, 0)), row(N_PROJ_PAD)],
        out_specs=out_specs,
        out_shape=out_shape,
        compiler_params=_params("parallel"),
        name="inproj",
    )(x, ln_g, ln_b, w, bias)


def _pad_rows(a, rows):
    if a.shape[0] == rows:
        return a
    return jnp.concatenate([a, jnp.zeros((rows - a.shape[0], a.shape[1]), a.dtype)], axis=0)


def _hgrn_kernel(C, Cb, l_real, a_ref, lb_ref, gain_ref, s0_ref, o_ref, sout_ref, st_ref):
    c = pl.program_id(1)

    @pl.when(c == 0)
    def _():
        st_ref[...] = s0_ref[...]

    W = REC_W
    a = _pad_rows(a_ref[...], C)
    q, fa, v, ga = a[:, 0:W], a[:, W:2 * W], a[:, 2 * W:3 * W], a[:, 3 * W:4 * W]
    lb = lb_ref[...]
    valid = (c * C + lax.broadcasted_iota(jnp.int32, (C, 1), 0)) < l_real
    f = lb + (1.0 - lb) * _sigmoid(fa)
    g = jnp.where(valid, jnp.log(f), 0.0)
    kk = jnp.where(valid, 1.0 - f, 0.0)
    b = _prefix_sum_rows(g)

    bd = _block_diag_mask(W, HEAD_W)
    bd_bf = bd.astype(BF16)
    hm = _head_lane_masks(W)

    nb = C // SUB
    b3, q3, k3, v3 = (z.reshape(nb, SUB, W) for z in (b, q, kk, v))
    t_in = lax.broadcasted_iota(jnp.int32, (nb, SUB, W), 1)
    o3 = jnp.zeros((nb, SUB, W), F32)
    for s in range(SUB):
        e = jnp.exp(jnp.minimum(b3 - b3[:, s:s + 1, :], 0.0))
        p = jnp.where(t_in >= s, e * q3 * k3[:, s:s + 1, :], 0.0)
        r = _dot(p.reshape(C, W).astype(BF16), bd_bf)
        o3 = o3 + r.reshape(nb, SUB, W) * v3[:, s:s + 1, :]
    o = o3.reshape(C, W)

    t_row = lax.broadcasted_iota(jnp.int32, (C, 1), 0)
    s_col = lax.broadcasted_iota(jnp.int32, (1, C), 1)
    att = [jnp.zeros((C, C), F32) for _ in range(N_HEADS)]
    m = SUB
    while m < C:
        blk = 2 * m
        ref = jnp.broadcast_to(b.reshape(C // blk, blk, W)[:, m - 1:m, :], (C // blk, blk, W)).reshape(C, W)
        later = (t_row % blk) >= m
        qt = jnp.where(later, q * jnp.exp(jnp.minimum(b - ref, 0.0)), 0.0)
        kt = jnp.where(later, 0.0, kk * jnp.exp(jnp.minimum(ref - b, 0.0))).astype(BF16)
        same = (t_row // blk) == (s_col // blk)
        for h in range(N_HEADS):
            sc = _nt((qt * hm[h]).astype(BF16), kt)
            att[h] = att[h] + jnp.where(same, sc, 0.0)
        m = blk
    if C > SUB:
        for h in range(N_HEADS):
            o = o + _dot(att[h].astype(BF16), (v * hm[h]).astype(BF16))

    st = st_ref[...]
    o = o + _nt((q * jnp.exp(b)).astype(BF16), st.astype(BF16))
    b_end = b[C - 1:C, :]
    ke = kk * jnp.exp(b_end - b)
    st_new = st * jnp.exp(b_end) + jnp.where(bd, _tn(v.astype(BF16), ke.astype(BF16)), 0.0)
    st_ref[...] = st_new
    sout_ref[...] = st_new

    ms = _dot((o * o).astype(BF16), bd_bf) * (1.0 / HEAD_W)
    y = o * lax.rsqrt(ms + NORM_EPS) * gain_ref[...] * _silu(ga)
    o_ref[...] = y[:Cb].astype(o_ref.dtype)


def _hgrn(a, lb, gain, s0, l_real):
    B, L, _ = a.shape
    Cb = min(REC_CHUNK, L)
    W = REC_W
    return pl.pallas_call(
        functools.partial(_hgrn_kernel, REC_CHUNK, Cb, l_real),
        grid=(B, L // Cb),
        in_specs=[pl.BlockSpec((None, Cb, 4 * W), lambda b, c: (b, c, 0)),
                  pl.BlockSpec((1, W), lambda b, c: (0, 0)),
                  pl.BlockSpec((1, W), lambda b, c: (0, 0)),
                  pl.BlockSpec((None, W, W), lambda b, c: (b, 0, 0))],
        out_specs=[pl.BlockSpec((None, Cb, W), lambda b, c: (b, c, 0)),
                   pl.BlockSpec((None, W, W), lambda b, c: (b, 0, 0))],
        out_shape=[jax.ShapeDtypeStruct((B, L, W), BF16), jax.ShapeDtypeStruct((B, W, W), F32)],
        scratch_shapes=[pltpu.VMEM((W, W), F32)],
        compiler_params=_params("parallel", "arbitrary"),
        name="hgrn",
    )(a, lb, gain, s0)


def _log_sigmoid(x):
    return jnp.minimum(x, 0.0) - jnp.log1p(jnp.exp(-jnp.abs(x)))


def _mlstm_kernel(C, Cb, l_real, cg_ref, gt_ref, cw_ref, cb_ref, bf_ref, gain_ref, c0_ref, n0_ref, m0_ref,
                  tail0_ref, o_ref, cout_ref, nout_ref, mout_ref, ct_ref, n_ref, m_ref, tail_ref):
    c = pl.program_id(1)

    @pl.when(c == 0)
    def _():
        ct_ref[...] = c0_ref[...]
        n_ref[...] = n0_ref[...]
        m_ref[...] = m0_ref[...]
        tail_ref[...] = tail0_ref[...]

    W = REC_W
    cg = _pad_rows(cg_ref[...], C)
    gt = _pad_rows(gt_ref[...], C)
    valid = (c * C + lax.broadcasted_iota(jnp.int32, (C, 1), 0)) < l_real

    qk_pre = cg[:, 0:2 * W]
    xp = jnp.concatenate([tail_ref[...], qk_pre], axis=0)
    conv = cb_ref[...]
    base = SUB - (CONV_TAPS - 1)
    for j in range(CONV_TAPS):
        conv = conv + cw_ref[j:j + 1, :] * xp[base + j:base + j + C, :]
    tail_ref[...] = qk_pre[C - SUB:C, :]
    qk = _silu(conv)
    qc = qk[:, 0:W]
    kc = qk[:, W:2 * W] * (HEAD_W ** -0.5)
    vc = cg[:, 2 * W:3 * W]
    oc = cg[:, 3 * W:4 * W]

    lf = jnp.where(valid, _log_sigmoid(gt + bf_ref[...]), 0.0)
    li = jnp.where(valid, gt, NEG)
    fcum = _prefix_sum_rows(lf)
    lane_r = lax.broadcasted_iota(jnp.int32, (ROUTE_LANES, W), 0)
    lane_c = lax.broadcasted_iota(jnp.int32, (ROUTE_LANES, W), 1) // HEAD_W
    li_b = _dot(li, (lane_r == lane_c).astype(F32), HIGHEST)
    f_b = _dot(fcum, (lane_r == lane_c + N_HEADS).astype(F32), HIGHEST)
    f_t = fcum.T
    li_t = li.T

    hm = _head_lane_masks(W)
    bd = _block_diag_mask(W, HEAD_W)
    causal = lax.broadcasted_iota(jnp.int32, (C, C), 0) >= lax.broadcasted_iota(jnp.int32, (C, C), 1)
    m_prev = m_ref[...]
    kc_bf = kc.astype(BF16)
    num = jnp.zeros((C, W), F32)
    den = jnp.zeros((C, W), F32)
    cs_b = jnp.zeros((C, W), F32)
    mt_b = jnp.zeros((C, W), F32)
    for h in range(N_HEADS):
        f_col = fcum[:, N_HEADS + h:N_HEADS + h + 1]
        log_d = jnp.where(causal, f_col - f_t[N_HEADS + h:N_HEADS + h + 1, :] + li_t[h:h + 1, :], NEG)
        log_s = f_col + m_prev[:, h * HEAD_W:h * HEAD_W + 1]
        m_t = jnp.maximum(jnp.max(log_d, axis=1, keepdims=True), log_s)
        w = _nt((qc * hm[h]).astype(BF16), kc_bf) * jnp.exp(log_d - m_t)
        num = num + _dot(w.astype(BF16), (vc * hm[h]).astype(BF16))
        den = den + jnp.sum(w, axis=1, keepdims=True) * hm[h]
        cs_b = cs_b + jnp.exp(log_s - m_t) * hm[h]
        mt_b = mt_b + m_t * hm[h]

    ct = ct_ref[...]
    n_row = n_ref[...]
    num = num + cs_b * _nt(qc.astype(BF16), ct.astype(BF16))
    den = den + cs_b * _dot((qc * n_row).astype(BF16), bd.astype(BF16))
    hh = num / jnp.maximum(jnp.abs(den), jnp.exp(-mt_b))

    m_new = mt_b[C - 1:C, :]
    f_end = f_b[C - 1:C, :]
    ws = jnp.exp(f_end - f_b + li_b - m_new)
    decay = jnp.exp(f_end + m_prev - m_new)
    ct_new = ct * decay + jnp.where(bd, _tn((vc * ws).astype(BF16), kc_bf), 0.0)
    n_new = decay * n_row + jnp.sum(ws * kc, axis=0, keepdims=True)
    ct_ref[...] = ct_new
    n_ref[...] = n_new
    m_ref[...] = m_new
    cout_ref[...] = ct_new
    nout_ref[...] = n_new
    mout_ref[...] = m_new

    ms = _dot((hh * hh).astype(BF16), bd.astype(BF16)) * (1.0 / HEAD_W)
    y = _sigmoid(oc) * (hh * lax.rsqrt(ms + NORM_EPS) * gain_ref[...])
    o_ref[...] = y[:Cb].astype(o_ref.dtype)


def _mlstm(cg, gates, conv_w, conv_b, bias_f, gain, c0, n0, m0, tail0, l_real):
    B, L, _ = cg.shape
    Cb = min(REC_CHUNK, L)
    W = REC_W
    const = lambda shape: pl.BlockSpec(shape, lambda b, c: (0,) * len(shape))
    per_b = lambda shape: pl.BlockSpec((None,) + shape, lambda b, c: (b,) + (0,) * len(shape))
    return pl.pallas_call(
        functools.partial(_mlstm_kernel, REC_CHUNK, Cb, l_real),
        grid=(B, L // Cb),
        in_specs=[pl.BlockSpec((None, Cb, 4 * W), lambda b, c: (b, c, 0)),
                  pl.BlockSpec((None, Cb, ROUTE_LANES), lambda b, c: (b, c, 0)),
                  const((CONV_TAPS, 2 * W)), const((1, 2 * W)), const((1, ROUTE_LANES)), const((1, W)),
                  per_b((W, W)), per_b((1, W)), per_b((1, W)), per_b((SUB, 2 * W))],
        out_specs=[pl.BlockSpec((None, Cb, W), lambda b, c: (b, c, 0)),
                   per_b((W, W)), per_b((1, W)), per_b((1, W))],
        out_shape=[jax.ShapeDtypeStruct((B, L, W), BF16), jax.ShapeDtypeStruct((B, W, W), F32),
                   jax.ShapeDtypeStruct((B, 1, W), F32), jax.ShapeDtypeStruct((B, 1, W), F32)],
        scratch_shapes=[pltpu.VMEM((W, W), F32), pltpu.VMEM((1, W), F32), pltpu.VMEM((1, W), F32),
                        pltpu.VMEM((SUB, 2 * W), F32)],
        compiler_params=_params("parallel", "arbitrary"),
        name="mlstm",
    )(cg, gates, conv_w, conv_b, bias_f, gain, c0, n0, m0, tail0)


def _diff_finish(o1, o2, lam, gain, out_scale):
    o = o1 - lam * o2
    ms = jnp.mean(o * o, axis=-1, keepdims=True)
    return out_scale * (o * lax.rsqrt(ms + NORM_EPS) * gain)


def _attn_kernel(T, out_scale, qi_ref, ki_ref, q_ref, k_ref, v_ref, lam_ref, gain_ref, o_ref,
                 qq_sc, m_sc, l_sc, acc_sc):
    step_id = pl.program_id(1)
    qi = qi_ref[step_id]
    ki = ki_ref[step_id]
    heads = [slice(h * ATT_W, (h + 1) * ATT_W) for h in range(N_HEADS)]
    rb = ATT_ROWS if (2 * T) % ATT_ROWS == 0 else T
    lane_tiles = T // ATT_W

    @pl.when(ki == 0)
    def _():
        lane = lax.broadcasted_iota(jnp.int32, (1, ATT_W), 1)
        for h in range(N_HEADS):
            q = q_ref[:, heads[h]]
            zero = jnp.zeros_like(q)
            qq_sc[h, 0:T, :] = jnp.where(lane < HEAD_W, q, zero)
            qq_sc[h, T:2 * T, :] = jnp.where(lane >= HEAD_W, q, zero)
        m_sc[...] = jnp.full(m_sc.shape, NEG, F32)
        l_sc[...] = jnp.zeros(l_sc.shape, F32)
        acc_sc[...] = jnp.zeros(acc_sc.shape, F32)

    def step(diagonal):
        for h in range(N_HEADS):
            k_h = k_ref[:, heads[h]]
            v_h = v_ref[:, heads[h]]
            for r0 in range(0, 2 * T, rb):
                rows = slice(r0, r0 + rb)
                s = _nt(qq_sc[h, rows, :], k_h)
                if diagonal:
                    row = (r0 + lax.broadcasted_iota(jnp.int32, (rb, 1), 0)) % T
                    col = lax.broadcasted_iota(jnp.int32, (1, T), 1)
                    s = jnp.where(row >= col, s, NEG)
                m_old = m_sc[h, rows, :]
                m_new = jnp.maximum(m_old, jnp.max(s, axis=1, keepdims=True))
                alpha = jnp.exp2(m_old - m_new)
                p = jnp.exp2(s - jnp.concatenate([m_new] * lane_tiles, axis=1))
                p_lanes = p[:, 0:ATT_W]
                for t in range(1, lane_tiles):
                    p_lanes = p_lanes + p[:, t * ATT_W:(t + 1) * ATT_W]
                l_sc[h, rows, :] = alpha * l_sc[h, rows, :] + p_lanes
                acc_sc[h, rows, :] = alpha * acc_sc[h, rows, :] + _dot(p.astype(BF16), v_h)
                m_sc[h, rows, :] = m_new

    @pl.when(ki < qi)
    def _():
        step(False)

    @pl.when(ki == qi)
    def _():
        step(True)
        for h in range(N_HEADS):
            on = acc_sc[h] / jnp.sum(l_sc[h], axis=1, keepdims=True)
            y = _diff_finish(on[:T], on[T:], lam_ref[...], gain_ref[:, heads[h]], out_scale)
            o_ref[:, heads[h]] = y.astype(o_ref.dtype)


def _attn_prompt(q, k, v, lam_row, gain, out_scale):
    B, L, _ = q.shape
    T = ATT_TILE if L % ATT_TILE == 0 else SEQ_TILE
    n = L // T
    pairs = [(a, b) for a in range(n) for b in range(a + 1)]
    qi_arr = jnp.asarray([a for a, _ in pairs], jnp.int32)
    ki_arr = jnp.asarray([b for _, b in pairs], jnp.int32)
    q_spec = pl.BlockSpec((None, T, ATT_ALL), lambda b, p, qi, ki: (b, qi[p], 0))
    kv_spec = pl.BlockSpec((None, T, ATT_ALL), lambda b, p, qi, ki: (b, ki[p], 0))
    grid_spec = pltpu.PrefetchScalarGridSpec(
        num_scalar_prefetch=2,
        grid=(B, len(pairs)),
        in_specs=[q_spec, kv_spec, kv_spec,
                  pl.BlockSpec((1, ATT_W), lambda b, p, qi, ki: (0, 0)),
                  pl.BlockSpec((1, ATT_ALL), lambda b, p, qi, ki: (0, 0))],
        out_specs=q_spec,
        scratch_shapes=[pltpu.VMEM((N_HEADS, 2 * T, ATT_W), BF16), pltpu.VMEM((N_HEADS, 2 * T, ATT_W), F32),
                        pltpu.VMEM((N_HEADS, 2 * T, ATT_W), F32), pltpu.VMEM((N_HEADS, 2 * T, ATT_W), F32)],
    )
    return pl.pallas_call(
        functools.partial(_attn_kernel, T, out_scale),
        grid_spec=grid_spec,
        out_shape=jax.ShapeDtypeStruct((B, L, ATT_ALL), BF16),
        compiler_params=_params("parallel", "arbitrary"),
        name="attn_prompt",
    )(qi_arr, ki_arr, q, k, v, lam_row, gain)


def _dec_attn_kernel(npg, ls, page, out_scale, pt_ref, q_ref, kn_ref, vn_ref, lam_ref, gain_ref, *refs):
    del pt_ref
    k_refs, v_refs = refs[:npg], refs[npg:2 * npg]
    o_ref, qs_sc, m_sc, l_sc, acc_sc = refs[2 * npg:]
    j = pl.program_id(1)
    rows = 2 * N_HEADS * ls
    cols = page * N_HEADS
    row_head = lax.broadcasted_iota(jnp.int32, (rows, 1), 0) // (2 * ls)

    @pl.when(j == 0)
    def _():
        q = q_ref[...].astype(F32)
        lane = lax.broadcasted_iota(jnp.int32, (1, ATT_W), 1)
        parts = []
        for h in range(N_HEADS):
            qh = q[:, h * ATT_W:(h + 1) * ATT_W]
            parts += [jnp.where(lane < HEAD_W, qh, 0.0), jnp.where(lane >= HEAD_W, qh, 0.0)]
        qs_sc[...] = jnp.concatenate(parts, axis=0).astype(BF16)
        m_sc[...] = jnp.full(m_sc.shape, NEG, F32)
        l_sc[...] = jnp.zeros(l_sc.shape, F32)
        acc_sc[...] = jnp.zeros(acc_sc.shape, F32)

    qs = qs_sc[...]

    def update(scores, values):
        m_old = m_sc[...]
        m_new = m_old
        for s in scores:
            m_new = jnp.maximum(m_new, jnp.max(s, axis=1, keepdims=True))
        alpha = jnp.exp2(m_old - m_new)
        l_new = alpha * l_sc[...]
        acc = alpha * acc_sc[...]
        for s, val in zip(scores, values):
            p = jnp.exp2(s - m_new)
            l_new = l_new + jnp.sum(p, axis=1, keepdims=True)
            acc = acc + _dot(p.astype(BF16), val)
        l_sc[...] = l_new
        acc_sc[...] = acc
        m_sc[...] = m_new

    own_head = (lax.broadcasted_iota(jnp.int32, (1, cols), 1) % N_HEADS) == row_head
    update([jnp.where(own_head, _nt(qs, kr[...].astype(BF16)), NEG) for kr in k_refs],
           [vr[...].astype(BF16) for vr in v_refs])

    @pl.when(j == pl.num_programs(1) - 1)
    def _():
        n_new = ls * N_HEADS
        kn = _pad_rows(kn_ref[...], ATT_W).astype(BF16)
        vn = _pad_rows(vn_ref[...], ATT_W).astype(BF16)
        col = lax.broadcasted_iota(jnp.int32, (1, ATT_W), 1)
        q_idx = lax.broadcasted_iota(jnp.int32, (rows, 1), 0) % ls
        ok = (col % N_HEADS == row_head) & (col // N_HEADS <= q_idx) & (col < n_new)
        update([jnp.where(ok, _nt(qs, kn), NEG)], [vn])
        on = acc_sc[...] / l_sc[...]
        gain = gain_ref[...]
        parts = []
        for h in range(N_HEADS):
            r0 = h * 2 * ls
            parts.append(_diff_finish(on[r0:r0 + ls], on[r0 + ls:r0 + 2 * ls], lam_ref[...],
                                      gain[:, h * ATT_W:(h + 1) * ATT_W], out_scale))
        o_ref[...] = jnp.concatenate(parts, axis=1).astype(o_ref.dtype)


def _attn_sample(q, k_new, v_new, cache_k, cache_v, page_table, layer, lam_row, gain, out_scale):
    B, ls, _ = q.shape
    n_pages = page_table.shape[1]
    page = cache_k.shape[2]
    cols = page * N_HEADS
    npg = PAGES_PER_STEP if n_pages % PAGES_PER_STEP == 0 else 1
    ck = cache_k.reshape(cache_k.shape[0], cache_k.shape[1], cols, ATT_W)
    cv = cache_v.reshape(cache_v.shape[0], cache_v.shape[1], cols, ATT_W)
    kn = k_new.reshape(B, ls * N_HEADS, ATT_W)
    vn = v_new.reshape(B, ls * N_HEADS, ATT_W)

    def page_spec(i):
        return pl.BlockSpec((None, None, cols, ATT_W), lambda b, j, pt: (layer, pt[b, j * npg + i], 0, 0))

    per_b = pl.BlockSpec((None, ls, ATT_ALL), lambda b, j, pt: (b, 0, 0))
    new_spec = pl.BlockSpec((None, ls * N_HEADS, ATT_W), lambda b, j, pt: (b, 0, 0))
    rows = 2 * N_HEADS * ls
    grid_spec = pltpu.PrefetchScalarGridSpec(
        num_scalar_prefetch=1,
        grid=(B, n_pages // npg),
        in_specs=[per_b, new_spec, new_spec,
                  pl.BlockSpec((1, ATT_W), lambda b, j, pt: (0, 0)),
                  pl.BlockSpec((1, ATT_ALL), lambda b, j, pt: (0, 0))]
                 + [page_spec(i) for i in range(npg)] * 2,
        out_specs=per_b,
        scratch_shapes=[pltpu.VMEM((rows, ATT_W), BF16), pltpu.VMEM((rows, 1), F32), pltpu.VMEM((rows, 1), F32),
                        pltpu.VMEM((rows, ATT_W), F32)],
    )
    return pl.pallas_call(
        functools.partial(_dec_attn_kernel, npg, ls, page, out_scale),
        grid_spec=grid_spec,
        out_shape=jax.ShapeDtypeStruct((B, ls, ATT_ALL), BF16),
        compiler_params=_params("parallel", "arbitrary"),
        name="attn_sample",
    )(page_table, q, kn, vn, lam_row, gain, *([ck] * npg), *([cv] * npg))


def _outproj_kernel(alpha, ma_ref, mb_ref, mc_ref, h_ref, wo_ref, g_ref, b_ref, wr_ref, br_ref,
                    h1_ref, h1t_ref, eid_ref, gate_ref):
    W = REC_W
    mix = (_dot(ma_ref[...], wo_ref[0:W, :]) + _dot(mb_ref[...], wo_ref[W:W + ATT_ALL, :])
           + _dot(mc_ref[...], wo_ref[W + ATT_ALL:2 * W + ATT_ALL, :]))
    h1 = _layernorm(alpha * h_ref[...] + mix, g_ref[...], b_ref[...])
    h1_ref[...] = h1
    _to_row_tiles(h1t_ref, h1)

    logits = _dot(h1.astype(BF16), wr_ref[...]) + br_ref[...]
    lane = lax.broadcasted_iota(jnp.int32, (1, ROUTE_LANES), 1)
    lane_f = lane.astype(F32)
    far = float(ROUTE_LANES)

    gl = jnp.where(lane < N_GROUPS, logits, NEG)
    g_max = jnp.max(gl, axis=1, keepdims=True)
    g_top = 1.0 / jnp.sum(jnp.exp(gl - g_max), axis=1, keepdims=True)
    g_idx = jnp.min(jnp.where(gl == g_max, lane_f, far), axis=1, keepdims=True)

    e_lane = lane_f - float(EXPERT_LANE0)
    in_group = (e_lane >= g_idx * GROUP_EXPERTS) & (e_lane < (g_idx + 1.0) * GROUP_EXPERTS)
    el = jnp.where(in_group, logits, NEG)
    e_max = jnp.max(el, axis=1, keepdims=True)
    e_sum = jnp.sum(jnp.exp(el - e_max), axis=1, keepdims=True)
    idx1 = jnp.min(jnp.where(el == e_max, lane_f, far), axis=1, keepdims=True)
    el2 = jnp.where(lane_f == idx1, NEG, el)
    e_max2 = jnp.max(el2, axis=1, keepdims=True)
    idx2 = jnp.min(jnp.where(el2 == e_max2, lane_f, far), axis=1, keepdims=True)
    p1 = 1.0 / e_sum
    p2 = jnp.exp(e_max2 - e_max) / e_sum
    tot = p1 + p2
    gate_ref[...] = jnp.where(lane == 0, g_top * (p1 / tot), jnp.where(lane == 1, g_top * (p2 / tot), 0.0))
    eid = jnp.where(lane == 0, idx1 - float(EXPERT_LANE0), jnp.where(lane == 1, idx2 - float(EXPERT_LANE0), 0.0))
    eid_ref[...] = eid.astype(jnp.int32)


def _outproj(ma, mb, mc, h, wo, g, b, wr, br, alpha):
    T, D = h.shape
    tm = min(ROW_TILE, T)
    tile = lambda n: pl.BlockSpec((tm, n), lambda i: (i, 0))
    const = lambda r, n: pl.BlockSpec((r, n), lambda i: (0, 0))
    return pl.pallas_call(
        functools.partial(_outproj_kernel, alpha),
        grid=(T // tm,),
        in_specs=[tile(REC_W), tile(ATT_ALL), tile(REC_W), tile(D), const(2 * REC_W + ATT_ALL, D),
                  const(1, D), const(1, D), const(D, ROUTE_LANES), const(1, ROUTE_LANES)],
        out_specs=[tile(D), pl.BlockSpec((tm, D // LANES, LANES), lambda i: (i, 0, 0)), tile(ROUTE_LANES),
                   tile(ROUTE_LANES)],
        out_shape=[jax.ShapeDtypeStruct((T, D), F32), jax.ShapeDtypeStruct((T, D // LANES, LANES), F32),
                   jax.ShapeDtypeStruct((T, ROUTE_LANES), jnp.int32),
                   jax.ShapeDtypeStruct((T, ROUTE_LANES), F32)],
        compiler_params=_params("parallel"),
        name="outproj",
    )(ma, mb, mc, h, wo, g, b, wr, br)


def _expert_kernel(row_ref, blke_ref, nval_ref, x_hbm, wg_ref, wu_ref, wd_ref, out_hbm,
                   xbuf, ybuf, gsem, ssem):
    del blke_ref
    i = pl.program_id(0)
    nblk = pl.num_programs(0)
    cur = i % 2

    def start_gather(blk, buf):
        @pl.when(nval_ref[blk] > 0)
        def _():
            def body(r, carry):
                tok = lax.shift_right_logical(row_ref[blk * MOE_ROWS + r], 1)
                pltpu.make_async_copy(x_hbm.at[pl.ds(tok, 1)], xbuf.at[buf, pl.ds(r, 1)], gsem.at[buf]).start()
                return carry
            lax.fori_loop(0, MOE_ROWS, body, 0, unroll=SUB)

    def start_scatter(blk, buf):
        def one(r):
            code = row_ref[blk * MOE_ROWS + r]
            pltpu.make_async_copy(ybuf.at[buf, pl.ds(r, 1)],
                                  out_hbm.at[code & 1, pl.ds(lax.shift_right_logical(code, 1), 1)],
                                  ssem.at[buf]).start()

        n = nval_ref[blk]
        full = lax.shift_right_logical(n, 3)

        def group(g, carry):
            for u in range(SUB):
                one(g * SUB + u)
            return carry
        lax.fori_loop(0, full, group, 0)

        def rest(r, carry):
            one(r)
            return carry
        lax.fori_loop(full * SUB, n, rest, 0)

    def wait_scatter(blk, buf):
        n = nval_ref[blk]

        @pl.when(n > 0)
        def _():
            pltpu.make_async_copy(ybuf.at[buf, pl.ds(0, n)], out_hbm.at[0, pl.ds(0, n)], ssem.at[buf]).wait()

    @pl.when(i == 0)
    def _():
        start_gather(0, 0)

    @pl.when(i + 1 < nblk)
    def _():
        start_gather(i + 1, 1 - cur)

    @pl.when(i >= 2)
    def _():
        wait_scatter(i - 2, cur)

    @pl.when(nval_ref[i] > 0)
    def _():
        pltpu.make_async_copy(x_hbm.at[pl.ds(0, MOE_ROWS)], xbuf.at[cur], gsem.at[cur]).wait()
        xb = _from_row_tiles(xbuf.at[cur]).astype(BF16)
        hid = _silu(_dot(xb, wg_ref[...])) * _dot(xb, wu_ref[...])
        _to_row_tiles(ybuf.at[cur], _dot(hid.astype(BF16), wd_ref[...]))
        start_scatter(i, cur)

    @pl.when(i == nblk - 1)
    def _():
        @pl.when(i >= 1)
        def _():
            wait_scatter(i - 1, 1 - cur)
        wait_scatter(i, cur)


def _experts(x_tiles, row_code, blk_e, blk_nval, wg, wu, wd):
    T, chunks, _ = x_tiles.shape
    D = chunks * LANES
    n_blocks = blk_e.shape[0]
    de = wg.shape[2]
    grid_spec = pltpu.PrefetchScalarGridSpec(
        num_scalar_prefetch=3,
        grid=(n_blocks,),
        in_specs=[pl.BlockSpec(memory_space=pl.ANY),
                  pl.BlockSpec((None, D, de), lambda i, code, be, nv: (be[i], 0, 0)),
                  pl.BlockSpec((None, D, de), lambda i, code, be, nv: (be[i], 0, 0)),
                  pl.BlockSpec((None, de, D), lambda i, code, be, nv: (be[i], 0, 0))],
        out_specs=pl.BlockSpec(memory_space=pl.ANY),
        scratch_shapes=[pltpu.VMEM((2, MOE_ROWS, chunks, LANES), F32), pltpu.VMEM((2, MOE_ROWS, chunks, LANES), F32),
                        pltpu.SemaphoreType.DMA((2,)), pltpu.SemaphoreType.DMA((2,))],
    )
    return pl.pallas_call(
        _expert_kernel,
        grid_spec=grid_spec,
        out_shape=jax.ShapeDtypeStruct((2, T, chunks, LANES), F32),
        compiler_params=_params("arbitrary"),
        name="experts",
    )(row_code, blk_e, blk_nval, x_tiles, wg, wu, wd)


def _dispatch(eid, T):
    K = 2
    A = T * K
    flat_e = eid[:, :K].reshape(-1)
    onehot = (flat_e[:, None] == jnp.arange(N_EXPERTS, dtype=jnp.int32)[None, :]).astype(F32)
    a_pad = -(-A // MOE_ROWS) * MOE_ROWS
    oh3 = jnp.pad(onehot, ((0, a_pad - A), (0, 0))).reshape(a_pad // MOE_ROWS, MOE_ROWS, N_EXPERTS)
    tril = jnp.tril(jnp.ones((MOE_ROWS, MOE_ROWS), F32))
    within = jnp.einsum('ts,bse->bte', tril, oh3, precision=HIGHEST)
    blk_tot = within[:, -1, :]
    before = jnp.cumsum(blk_tot, axis=0) - blk_tot
    rank = jnp.sum((within + before[:, None, :]) * oh3, axis=-1).reshape(a_pad)[:A] - 1.0
    counts = jnp.sum(blk_tot, axis=0).astype(jnp.int32)
    padded = (counts + MOE_ROWS - 1) // MOE_ROWS * MOE_ROWS
    pad_end = jnp.cumsum(padded)
    pad_start = pad_end - padded
    dest = jnp.sum(onehot * pad_start.astype(F32)[None, :], axis=-1) + rank
    n_blocks = -(-A // MOE_ROWS) + N_EXPERTS
    R = n_blocks * MOE_ROWS
    row_code = jnp.zeros((R,), jnp.int32).at[dest.astype(jnp.int32)].set(jnp.arange(A, dtype=jnp.int32))
    blk_start = jnp.arange(n_blocks, dtype=jnp.int32) * MOE_ROWS
    blk_e = jnp.minimum(jnp.sum((pad_end[None, :] <= blk_start[:, None]).astype(jnp.int32), axis=1), N_EXPERTS - 1)
    blk_nval = jnp.clip((pad_start + counts)[blk_e] - blk_start, 0, MOE_ROWS).astype(jnp.int32)
    return row_code, blk_e.astype(jnp.int32), blk_nval


def _combine_kernel(alpha, h_ref, y_ref, gate_ref, g_ref, b_ref, o_ref):
    gate = gate_ref[...]
    y = gate[:, 0:1] * _from_row_tiles(y_ref.at[0]) + gate[:, 1:2] * _from_row_tiles(y_ref.at[1])
    o_ref[...] = _layernorm(alpha * h_ref[...] + y, g_ref[...], b_ref[...])


def _combine(h1, slots, gates, g, b, alpha):
    T, D = h1.shape
    tm = min(ROW_TILE, T)
    return pl.pallas_call(
        functools.partial(_combine_kernel, alpha),
        grid=(T // tm,),
        in_specs=[pl.BlockSpec((tm, D), lambda i: (i, 0)), pl.BlockSpec((2, tm, D // LANES, LANES), lambda i: (0, i, 0, 0)),
                  pl.BlockSpec((tm, ROUTE_LANES), lambda i: (i, 0)),
                  pl.BlockSpec((1, D), lambda i: (0, 0)), pl.BlockSpec((1, D), lambda i: (0, 0))],
        out_specs=pl.BlockSpec((tm, D), lambda i: (i, 0)),
        out_shape=jax.ShapeDtypeStruct((T, D), F32),
        compiler_params=_params("parallel"),
        name="combine",
    )(h1, slots, gates, g, b)


def _block_diag(blocks):
    B, H, n, _ = blocks.shape
    eye = jnp.eye(H, dtype=blocks.dtype)
    return jnp.einsum('bhij,hg->bhigj', blocks, eye).reshape(B, H * n, H * n)


def _diag_blocks(mat, H):
    B, n = mat.shape[0], mat.shape[1] // H
    m5 = mat.reshape(B, H, n, H, n)
    return jnp.stack([m5[:, h, :, h, :] for h in range(H)], axis=1)


def _layer(x, do_ln, B, L, l_real, lw, state, past):
    n_seq, l_out = (B, l_real) if past is None else (1, B * L)
    outs = _inproj(x, lw['ln_g'], lw['ln_b'], lw['w_in'], lw['b_in'], do_ln, n_seq, l_out)
    a, qb, kb, vb, cg, gates, kb16, vb16 = outs[:8]
    h = outs[8] if do_ln else x
    s3 = lambda z: z.reshape(B, L, z.shape[-1])

    mix_a, st_a = _hgrn(s3(a), lw['lb'], lw['norm_a'], state['hgrn'], l_real)
    mix_c, ct, n_c, m_c = _mlstm(s3(cg), s3(gates), lw['conv_w'], lw['conv_b'], lw['b_f'], lw['norm_c'],
                                 state['c'], state['n'], state['m'], state['tail'], l_real)
    if past is None:
        mix_b = _attn_prompt(s3(qb), s3(kb16), s3(vb16), lw['lam'], lw['norm_b'], lw['out_scale'])
    else:
        mix_b = _attn_sample(s3(qb), kb, vb, past[0], past[1], past[2], lw['layer'], lw['lam'],
                             lw['norm_b'], lw['out_scale'])

    flat = lambda z: z.reshape(B * L, z.shape[-1])
    h1, h1_tiles, eid, gate = _outproj(flat(mix_a), flat(mix_b), flat(mix_c), h, lw['w_out'], lw['ln1_g'], lw['ln1_b'],
                             lw['w_route'], lw['b_route'], lw['alpha'])
    row_code, blk_e, blk_nval = _dispatch(eid, B * L)
    slots = _experts(h1_tiles, row_code, blk_e, blk_nval, lw['w_eg'], lw['w_eu'], lw['w_ed'])
    h2 = _combine(h1, slots, gate, lw['ln2_g'], lw['ln2_b'], lw['alpha'])

    new_state = {
        'k': kb.reshape(B, l_real, N_HEADS, ATT_W), 'v': vb.reshape(B, l_real, N_HEADS, ATT_W),
        'hgrn': jnp.swapaxes(_diag_blocks(st_a, N_HEADS), -1, -2),
        'c': _diag_blocks(ct, N_HEADS),
        'n': n_c.reshape(B, N_HEADS, HEAD_W),
        'm': m_c.reshape(B, N_HEADS, HEAD_W)[:, :, 0],
        'conv': s3(cg)[:, l_real - (CONV_TAPS - 1):l_real, :2 * REC_W],
    }
    return h2, new_state


def kernel(x_prompt, x_sample, cache_k, cache_v, page_table, state_hgrn, state_mlstm_c, state_mlstm_n, state_mlstm_m, state_mlstm_conv, meta_tokens, ln_emb_g, ln_emb_b, w_in, b_in, b_mlstm_f, conv_w, conv_b, hgrn_lb_logits, lambda_q1, lambda_k1, lambda_q2, lambda_k2, norm_a, norm_b, norm_c, w_out, ln1_g, ln1_b, w_router_group, b_router_group, w_router_expert, b_router_expert, w_exp_gate, w_exp_up, w_exp_down, ln2_g, ln2_b):
    depth = w_in.shape[0]
    Bp, Lq, D = x_prompt.shape
    Bs, Ls, _ = x_sample.shape
    n_meta = meta_tokens.shape[0]
    l_real = n_meta + Lq
    Lp = -(-l_real // SEQ_TILE) * SEQ_TILE
    alpha = (2.0 * depth) ** 0.25
    row = lambda z: z.reshape(1, -1).astype(F32)

    meta = jnp.broadcast_to(meta_tokens[None].astype(x_prompt.dtype), (Bp, n_meta, D))
    xp = jnp.concatenate([meta, x_prompt, jnp.zeros((Bp, Lp - l_real, D), x_prompt.dtype)], axis=1)
    hp = xp.reshape(Bp * Lp, D)
    hs = x_sample.reshape(Bs * Ls, D)

    lb_sm = jax.nn.softmax(hgrn_lb_logits.astype(F32), axis=0)
    lb_all = jnp.cumsum(lb_sm, axis=0) - lb_sm[0]

    zeros_p = {
        'hgrn': jnp.zeros((Bp, REC_W, REC_W), F32), 'c': jnp.zeros((Bp, REC_W, REC_W), F32),
        'n': jnp.zeros((Bp, 1, REC_W), F32), 'm': jnp.zeros((Bp, 1, REC_W), F32),
        'tail': jnp.zeros((Bp, SUB, 2 * REC_W), F32),
    }

    res_p, res_s = [], []
    for l in range(depth):
        lam_init = 0.8 - 0.6 * math.exp(-0.3 * l)
        lam = (jnp.exp(jnp.sum(lambda_q1[l].astype(F32) * lambda_k1[l].astype(F32)))
               - jnp.exp(jnp.sum(lambda_q2[l].astype(F32) * lambda_k2[l].astype(F32))) + lam_init)
        pad_cols = N_PROJ_PAD - N_PROJ_RAW
        bias_f = jnp.zeros((ROUTE_LANES,), F32).at[N_HEADS:2 * N_HEADS].set(b_mlstm_f[l].astype(F32))
        w_route = jnp.concatenate([w_router_group[l], w_router_expert[l]], axis=1)
        b_route = jnp.concatenate([b_router_group[l], b_router_expert[l]])
        r_pad = ROUTE_LANES - w_route.shape[1]
        lw = {
            'layer': l, 'alpha': alpha, 'out_scale': 1.0 - lam_init,
            'ln_g': row(ln_emb_g), 'ln_b': row(ln_emb_b),
            'w_in': jnp.pad(w_in[l], ((0, 0), (0, pad_cols))).astype(BF16),
            'b_in': row(jnp.pad(b_in[l], (0, pad_cols))),
            'lb': row(lb_all[l]), 'norm_a': row(norm_a[l]), 'norm_b': row(norm_b[l]), 'norm_c': row(norm_c[l]),
            'lam': jnp.full((1, ATT_W), lam, F32),
            'conv_w': conv_w[l].astype(F32), 'conv_b': row(conv_b[l]), 'b_f': row(bias_f),
            'w_out': w_out[l].astype(BF16), 'ln1_g': row(ln1_g[l]), 'ln1_b': row(ln1_b[l]),
            'w_route': jnp.pad(w_route, ((0, 0), (0, r_pad))).astype(BF16), 'b_route': row(jnp.pad(b_route, (0, r_pad))),
            'w_eg': w_exp_gate[l].astype(BF16), 'w_eu': w_exp_up[l].astype(BF16), 'w_ed': w_exp_down[l].astype(BF16),
            'ln2_g': row(ln2_g[l]), 'ln2_b': row(ln2_b[l]),
        }
        hp, st_p = _layer(hp, l == 0, Bp, Lp, l_real, lw, zeros_p, None)

        state_s = {
            'hgrn': _block_diag(jnp.swapaxes(state_hgrn[l].astype(F32), -1, -2)),
            'c': _block_diag(state_mlstm_c[l].astype(F32)),
            'n': state_mlstm_n[l].astype(F32).reshape(Bs, 1, REC_W),
            'm': jnp.repeat(state_mlstm_m[l].astype(F32), HEAD_W, axis=-1).reshape(Bs, 1, REC_W),
            'tail': jnp.pad(state_mlstm_conv[l].astype(F32), ((0, 0), (SUB - (CONV_TAPS - 1), 0), (0, 0))),
        }
        hs, st_s = _layer(hs, l == 0, Bs, Ls, Ls, lw, state_s, (cache_k, cache_v, page_table))
        res_p.append(st_p)
        res_s.append(st_s)

    y_prompt = hp.reshape(Bp, Lp, D)[:, n_meta:l_real]
    y_sample = hs.reshape(Bs, Ls, D)
    stack = lambda res, key, f=lambda z: z: jnp.stack([f(r[key]) for r in res])
    return (y_prompt, y_sample,
            stack(res_p, 'k'), stack(res_p, 'v'), stack(res_s, 'k'), stack(res_s, 'v'),
            stack(res_p, 'hgrn'), stack(res_s, 'hgrn'),
            stack(res_p, 'c'), stack(res_s, 'c'),
            stack(res_p, 'n'), stack(res_s, 'n'),
            stack(res_p, 'm'), stack(res_s, 'm'),
            stack(res_p, 'conv'), stack(res_s, 'conv'))
```

```python
import functools
import math

import jax
import jax.numpy as jnp
from jax import lax
from jax.experimental import pallas as pl
from jax.experimental.pallas import tpu as pltpu

F32 = jnp.float32
BF16 = jnp.bfloat16
HIGHEST = lax.Precision.HIGHEST

N_META = 16
N_HEADS = 4
HEAD_W = 64
ATT_W = 2 * HEAD_W
REC_W = N_HEADS * HEAD_W
ATT_ALL = N_HEADS * ATT_W
CONV_TAPS = 4
N_GROUPS = 4
GROUP_EXPERTS = 8
N_EXPERTS = N_GROUPS * GROUP_EXPERTS
LANES = 128
ROUTE_LANES = LANES
EXPERT_LANE0 = N_GROUPS
MOE_ROWS = 128
LN_EPS = 1e-5
NORM_EPS = 1e-6
NEG = -1e30
LOG2_E = 1.4426950408889634

SEQ_TILE = 128
REC_CHUNK = 128
SUB = 8
ATT_TILE = 384
ATT_ROWS = 768
PAGES_PER_STEP = 16
ROW_TILE = 256
INPROJ_TILES = (384, 256, 128)
VMEM_LIMIT = 48 * 1024 * 1024

IN_WIDTHS = (4 * REC_W, ATT_ALL, ATT_ALL, ATT_ALL, 4 * REC_W, ROUTE_LANES)
N_PROJ_RAW = 4 * REC_W + 3 * ATT_ALL + 4 * REC_W + 2 * N_HEADS
N_PROJ_PAD = sum(IN_WIDTHS)


def _nt(a, b):
    return lax.dot_general(a, b, (((1,), (1,)), ((), ())), preferred_element_type=F32)


def _tn(a, b):
    return lax.dot_general(a, b, (((0,), (0,)), ((), ())), preferred_element_type=F32)


def _dot(a, b, precision=None):
    return jnp.dot(a, b, preferred_element_type=F32, precision=precision)


def _layernorm(x, g, b):
    mu = jnp.mean(x, axis=-1, keepdims=True)
    xc = x - mu
    var = jnp.mean(xc * xc, axis=-1, keepdims=True)
    return xc * lax.rsqrt(var + LN_EPS) * g + b


def _sigmoid(x):
    return 1.0 / (1.0 + jnp.exp(-x))


def _silu(x):
    return x * _sigmoid(x)


def _head_lane_masks(width):
    lane = lax.broadcasted_iota(jnp.int32, (1, width), 1)
    per = width // N_HEADS
    return [(lane // per == h).astype(F32) for h in range(N_HEADS)]


def _block_diag_mask(n, blk):
    r = lax.broadcasted_iota(jnp.int32, (n, n), 0) // blk
    c = lax.broadcasted_iota(jnp.int32, (n, n), 1) // blk
    return r == c


def _prefix_sum_rows(x):
    n = x.shape[0]
    r = lax.broadcasted_iota(jnp.int32, (n, n), 0)
    c = lax.broadcasted_iota(jnp.int32, (n, n), 1)
    tril = (r >= c).astype(BF16)
    hi = x.astype(BF16)
    rest = x - hi.astype(F32)
    mid = rest.astype(BF16)
    lo = (rest - mid.astype(F32)).astype(BF16)
    return _dot(tril, hi) + (_dot(tril, mid) + _dot(tril, lo))


def _to_row_tiles(ref_at, x):
    for c in range(x.shape[1] // LANES):
        ref_at[:, c, :] = x[:, c * LANES:(c + 1) * LANES]


def _from_row_tiles(ref_at):
    return jnp.concatenate([ref_at[:, c, :] for c in range(ref_at.shape[1])], axis=1)


def _params(*sem):
    return pltpu.CompilerParams(dimension_semantics=sem, vmem_limit_bytes=VMEM_LIMIT)


def _inproj_kernel(do_ln, x_ref, g_ref, b_ref, w_ref, bias_ref, *outs):
    x = x_ref[...]
    if do_ln:
        x = _layernorm(x, g_ref[...], b_ref[...])
        outs[-1][...] = x
    xb = x.astype(BF16)
    col = 0
    for idx, width in enumerate(IN_WIDTHS):
        acc = _dot(xb, w_ref[:, col:col + width]) + bias_ref[:, col:col + width]
        if idx == 1:
            acc = acc * (HEAD_W ** -0.5 * LOG2_E)
        if idx in (2, 3):
            _to_row_tiles(outs[idx], acc)
            outs[len(IN_WIDTHS) + idx - 2][...] = acc.astype(BF16)
        else:
            outs[idx][...] = acc.astype(outs[idx].dtype)
        col += width


def _inproj(x, ln_g, ln_b, w, bias, do_ln, n_seq, l_out):
    T, D = x.shape
    L = T // n_seq
    tm = next((t for t in INPROJ_TILES if L % t == 0), L)
    nj = L // tm
    widths = IN_WIDTHS + (ATT_ALL, ATT_ALL)
    dtypes = (F32, BF16, F32, F32, F32, F32, BF16, BF16)
    out_shape = [jax.ShapeDtypeStruct((T, wd), dt) for wd, dt in zip(widths, dtypes)]
    out_specs = [pl.BlockSpec((tm, wd), lambda i: (i, 0)) for wd in widths]
    for idx in (2, 3):
        out_shape[idx] = jax.ShapeDtypeStruct((n_seq, l_out, N_HEADS, ATT_W), F32)
        out_specs[idx] = pl.BlockSpec((None, tm, N_HEADS, ATT_W), lambda i: (i // nj, i % nj, 0, 0))
    if do_ln:
        out_shape.append(jax.ShapeDtypeStruct((T, D), F32))
        out_specs.append(pl.BlockSpec((tm, D), lambda i: (i, 0)))
    row = lambda n: pl.BlockSpec((1, n), lambda i: (0, 0))
    return pl.pallas_call(
        functools.partial(_inproj_kernel, do_ln),
        grid=(T // tm,),
        in_specs=[pl.BlockSpec((tm, D), lambda i: (i, 0)), row(D), row(D),
                  pl.BlockSpec((D, N_PROJ_PAD), lambda i: (0, 0)), row(N_PROJ_PAD)],
        out_specs=out_specs,
        out_shape=out_shape,
        compiler_params=_params("parallel"),
        name="inproj",
    )(x, ln_g, ln_b, w, bias)


def _pad_rows(a, rows):
    if a.shape[0] == rows:
        return a
    return jnp.concatenate([a, jnp.zeros((rows - a.shape[0], a.shape[1]), a.dtype)], axis=0)


def _hgrn_kernel(C, Cb, l_real, a_ref, lb_ref, gain_ref, s0_ref, o_ref, sout_ref, st_ref):
    c = pl.program_id(1)

    @pl.when(c == 0)
    def _():
        st_ref[...] = s0_ref[...]

    W = REC_W
    a = _pad_rows(a_ref[...], C)
    q, fa, v, ga = a[:, 0:W], a[:, W:2 * W], a[:, 2 * W:3 * W], a[:, 3 * W:4 * W]
    lb = lb_ref[...]
    valid = (c * C + lax.broadcasted_iota(jnp.int32, (C, 1), 0)) < l_real
    f = lb + (1.0 - lb) * _sigmoid(fa)
    g = jnp.where(valid, jnp.log(f), 0.0)
    kk = jnp.where(valid, 1.0 - f, 0.0)
    b = _prefix_sum_rows(g)

    bd = _block_diag_mask(W, HEAD_W)
    bd_bf = bd.astype(BF16)
    hm = _head_lane_masks(W)

    nb = C // SUB
    b3, q3, k3, v3 = (z.reshape(nb, SUB, W) for z in (b, q, kk, v))
    t_in = lax.broadcasted_iota(jnp.int32, (nb, SUB, W), 1)
    o3 = jnp.zeros((nb, SUB, W), F32)
    for s in range(SUB):
        e = jnp.exp(jnp.minimum(b3 - b3[:, s:s + 1, :], 0.0))
        p = jnp.where(t_in >= s, e * q3 * k3[:, s:s + 1, :], 0.0)
        r = _dot(p.reshape(C, W).astype(BF16), bd_bf)
        o3 = o3 + r.reshape(nb, SUB, W) * v3[:, s:s + 1, :]
    o = o3.reshape(C, W)

    t_row = lax.broadcasted_iota(jnp.int32, (C, 1), 0)
    s_col = lax.broadcasted_iota(jnp.int32, (1, C), 1)
    att = [jnp.zeros((C, C), F32) for _ in range(N_HEADS)]
    m = SUB
    while m < C:
        blk = 2 * m
        ref = jnp.broadcast_to(b.reshape(C // blk, blk, W)[:, m - 1:m, :], (C // blk, blk, W)).reshape(C, W)
        later = (t_row % blk) >= m
        qt = jnp.where(later, q * jnp.exp(jnp.minimum(b - ref, 0.0)), 0.0)
        kt = jnp.where(later, 0.0, kk * jnp.exp(jnp.minimum(ref - b, 0.0))).astype(BF16)
        same = (t_row // blk) == (s_col // blk)
        for h in range(N_HEADS):
            sc = _nt((qt * hm[h]).astype(BF16), kt)
            att[h] = att[h] + jnp.where(same, sc, 0.0)
        m = blk
    if C > SUB:
        for h in range(N_HEADS):
            o = o + _dot(att[h].astype(BF16), (v * hm[h]).astype(BF16))

    st = st_ref[...]
    o = o + _nt((q * jnp.exp(b)).astype(BF16), st.astype(BF16))
    b_end = b[C - 1:C, :]
    ke = kk * jnp.exp(b_end - b)
    st_new = st * jnp.exp(b_end) + jnp.where(bd, _tn(v.astype(BF16), ke.astype(BF16)), 0.0)
    st_ref[...] = st_new
    sout_ref[...] = st_new

    ms = _dot((o * o).astype(BF16), bd_bf) * (1.0 / HEAD_W)
    y = o * lax.rsqrt(ms + NORM_EPS) * gain_ref[...] * _silu(ga)
    o_ref[...] = y[:Cb].astype(o_ref.dtype)


def _hgrn(a, lb, gain, s0, l_real):
    B, L, _ = a.shape
    Cb = min(REC_CHUNK, L)
    W = REC_W
    return pl.pallas_call(
        functools.partial(_hgrn_kernel, REC_CHUNK, Cb, l_real),
        grid=(B, L // Cb),
        in_specs=[pl.BlockSpec((None, Cb, 4 * W), lambda b, c: (b, c, 0)),
                  pl.BlockSpec((1, W), lambda b, c: (0, 0)),
                  pl.BlockSpec((1, W), lambda b, c: (0, 0)),
                  pl.BlockSpec((None, W, W), lambda b, c: (b, 0, 0))],
        out_specs=[pl.BlockSpec((None, Cb, W), lambda b, c: (b, c, 0)),
                   pl.BlockSpec((None, W, W), lambda b, c: (b, 0, 0))],
        out_shape=[jax.ShapeDtypeStruct((B, L, W), BF16), jax.ShapeDtypeStruct((B, W, W), F32)],
        scratch_shapes=[pltpu.VMEM((W, W), F32)],
        compiler_params=_params("parallel", "arbitrary"),
        name="hgrn",
    )(a, lb, gain, s0)


def _log_sigmoid(x):
    return jnp.minimum(x, 0.0) - jnp.log1p(jnp.exp(-jnp.abs(x)))


def _mlstm_kernel(C, Cb, l_real, cg_ref, gt_ref, cw_ref, cb_ref, bf_ref, gain_ref, c0_ref, n0_ref, m0_ref,
                  tail0_ref, o_ref, cout_ref, nout_ref, mout_ref, ct_ref, n_ref, m_ref, tail_ref):
    c = pl.program_id(1)

    @pl.when(c == 0)
    def _():
        ct_ref[...] = c0_ref[...]
        n_ref[...] = n0_ref[...]
        m_ref[...] = m0_ref[...]
        tail_ref[...] = tail0_ref[...]

    W = REC_W
    cg = _pad_rows(cg_ref[...], C)
    gt = _pad_rows(gt_ref[...], C)
    valid = (c * C + lax.broadcasted_iota(jnp.int32, (C, 1), 0)) < l_real

    qk_pre = cg[:, 0:2 * W]
    xp = jnp.concatenate([tail_ref[...], qk_pre], axis=0)
    conv = cb_ref[...]
    base = SUB - (CONV_TAPS - 1)
    for j in range(CONV_TAPS):
        conv = conv + cw_ref[j:j + 1, :] * xp[base + j:base + j + C, :]
    tail_ref[...] = qk_pre[C - SUB:C, :]
    qk = _silu(conv)
    qc = qk[:, 0:W]
    kc = qk[:, W:2 * W] * (HEAD_W ** -0.5)
    vc = cg[:, 2 * W:3 * W]
    oc = cg[:, 3 * W:4 * W]

    lf = jnp.where(valid, _log_sigmoid(gt + bf_ref[...]), 0.0)
    li = jnp.where(valid, gt, NEG)
    fcum = _prefix_sum_rows(lf)
    lane_r = lax.broadcasted_iota(jnp.int32, (ROUTE_LANES, W), 0)
    lane_c = lax.broadcasted_iota(jnp.int32, (ROUTE_LANES, W), 1) // HEAD_W
    li_b = _dot(li, (lane_r == lane_c).astype(F32), HIGHEST)
    f_b = _dot(fcum, (lane_r == lane_c + N_HEADS).astype(F32), HIGHEST)
    f_t = fcum.T
    li_t = li.T

    hm = _head_lane_masks(W)
    bd = _block_diag_mask(W, HEAD_W)
    causal = lax.broadcasted_iota(jnp.int32, (C, C), 0) >= lax.broadcasted_iota(jnp.int32, (C, C), 1)
    m_prev = m_ref[...]
    kc_bf = kc.astype(BF16)
    num = jnp.zeros((C, W), F32)
    den = jnp.zeros((C, W), F32)
    cs_b = jnp.zeros((C, W), F32)
    mt_b = jnp.zeros((C, W), F32)
    for h in range(N_HEADS):
        f_col = fcum[:, N_HEADS + h:N_HEADS + h + 1]
        log_d = jnp.where(causal, f_col - f_t[N_HEADS + h:N_HEADS + h + 1, :] + li_t[h:h + 1, :], NEG)
        log_s = f_col + m_prev[:, h * HEAD_W:h * HEAD_W + 1]
        m_t = jnp.maximum(jnp.max(log_d, axis=1, keepdims=True), log_s)
        w = _nt((qc * hm[h]).astype(BF16), kc_bf) * jnp.exp(log_d - m_t)
        num = num + _dot(w.astype(BF16), (vc * hm[h]).astype(BF16))
        den = den + jnp.sum(w, axis=1, keepdims=True) * hm[h]
        cs_b = cs_b + jnp.exp(log_s - m_t) * hm[h]
        mt_b = mt_b + m_t * hm[h]

    ct = ct_ref[...]
    n_row = n_ref[...]
    num = num + cs_b * _nt(qc.astype(BF16), ct.astype(BF16))
    den = den + cs_b * _dot((qc * n_row).astype(BF16), bd.astype(BF16))
    hh = num / jnp.maximum(jnp.abs(den), jnp.exp(-mt_b))

    m_new = mt_b[C - 1:C, :]
    f_end = f_b[C - 1:C, :]
    ws = jnp.exp(f_end - f_b + li_b - m_new)
    decay = jnp.exp(f_end + m_prev - m_new)
    ct_new = ct * decay + jnp.where(bd, _tn((vc * ws).astype(BF16), kc_bf), 0.0)
    n_new = decay * n_row + jnp.sum(ws * kc, axis=0, keepdims=True)
    ct_ref[...] = ct_new
    n_ref[...] = n_new
    m_ref[...] = m_new
    cout_ref[...] = ct_new
    nout_ref[...] = n_new
    mout_ref[...] = m_new

    ms = _dot((hh * hh).astype(BF16), bd.astype(BF16)) * (1.0 / HEAD_W)
    y = _sigmoid(oc) * (hh * lax.rsqrt(ms + NORM_EPS) * gain_ref[...])
    o_ref[...] = y[:Cb].astype(o_ref.dtype)


def _mlstm(cg, gates, conv_w, conv_b, bias_f, gain, c0, n0, m0, tail0, l_real):
    B, L, _ = cg.shape
    Cb = min(REC_CHUNK, L)
    W = REC_W
    const = lambda shape: pl.BlockSpec(shape, lambda b, c: (0,) * len(shape))
    per_b = lambda shape: pl.BlockSpec((None,) + shape, lambda b, c: (b,) + (0,) * len(shape))
    return pl.pallas_call(
        functools.partial(_mlstm_kernel, REC_CHUNK, Cb, l_real),
        grid=(B, L // Cb),
        in_specs=[pl.BlockSpec((None, Cb, 4 * W), lambda b, c: (b, c, 0)),
                  pl.BlockSpec((None, Cb, ROUTE_LANES), lambda b, c: (b, c, 0)),
                  const((CONV_TAPS, 2 * W)), const((1, 2 * W)), const((1, ROUTE_LANES)), const((1, W)),
                  per_b((W, W)), per_b((1, W)), per_b((1, W)), per_b((SUB, 2 * W))],
        out_specs=[pl.BlockSpec((None, Cb, W), lambda b, c: (b, c, 0)),
                   per_b((W, W)), per_b((1, W)), per_b((1, W))],
        out_shape=[jax.ShapeDtypeStruct((B, L, W), BF16), jax.ShapeDtypeStruct((B, W, W), F32),
                   jax.ShapeDtypeStruct((B, 1, W), F32), jax.ShapeDtypeStruct((B, 1, W), F32)],
        scratch_shapes=[pltpu.VMEM((W, W), F32), pltpu.VMEM((1, W), F32), pltpu.VMEM((1, W), F32),
                        pltpu.VMEM((SUB, 2 * W), F32)],
        compiler_params=_params("parallel", "arbitrary"),
        name="mlstm",
    )(cg, gates, conv_w, conv_b, bias_f, gain, c0, n0, m0, tail0)


def _diff_finish(o1, o2, lam, gain, out_scale):
    o = o1 - lam * o2
    ms = jnp.mean(o * o, axis=-1, keepdims=True)
    return out_scale * (o * lax.rsqrt(ms + NORM_EPS) * gain)


def _attn_kernel(T, out_scale, qi_ref, ki_ref, q_ref, k_ref, v_ref, lam_ref, gain_ref, o_ref,
                 qq_sc, m_sc, l_sc, acc_sc):
    step_id = pl.program_id(1)
    qi = qi_ref[step_id]
    ki = ki_ref[step_id]
    heads = [slice(h * ATT_W, (h + 1) * ATT_W) for h in range(N_HEADS)]
    rb = ATT_ROWS if (2 * T) % ATT_ROWS == 0 else T
    lane_tiles = T // ATT_W

    @pl.when(ki == 0)
    def _():
        lane = lax.broadcasted_iota(jnp.int32, (1, ATT_W), 1)
        for h in range(N_HEADS):
            q = q_ref[:, heads[h]]
            zero = jnp.zeros_like(q)
            qq_sc[h, 0:T, :] = jnp.where(lane < HEAD_W, q, zero)
            qq_sc[h, T:2 * T, :] = jnp.where(lane >= HEAD_W, q, zero)
        m_sc[...] = jnp.full(m_sc.shape, NEG, F32)
        l_sc[...] = jnp.zeros(l_sc.shape, F32)
        acc_sc[...] = jnp.zeros(acc_sc.shape, F32)

    def step(diagonal):
        for h in range(N_HEADS):
            k_h = k_ref[:, heads[h]]
            v_h = v_ref[:, heads[h]]
            for r0 in range(0, 2 * T, rb):
                rows = slice(r0, r0 + rb)
                s = _nt(qq_sc[h, rows, :], k_h)
                if diagonal:
                    row = (r0 + lax.broadcasted_iota(jnp.int32, (rb, 1), 0)) % T
                    col = lax.broadcasted_iota(jnp.int32, (1, T), 1)
                    s = jnp.where(row >= col, s, NEG)
                m_old = m_sc[h, rows, :]
                m_new = jnp.maximum(m_old, jnp.max(s, axis=1, keepdims=True))
                alpha = jnp.exp2(m_old - m_new)
                p = jnp.exp2(s - jnp.concatenate([m_new] * lane_tiles, axis=1))
                p_lanes = p[:, 0:ATT_W]
                for t in range(1, lane_tiles):
                    p_lanes = p_lanes + p[:, t * ATT_W:(t + 1) * ATT_W]
                l_sc[h, rows, :] = alpha * l_sc[h, rows, :] + p_lanes
                acc_sc[h, rows, :] = alpha * acc_sc[h, rows, :] + _dot(p.astype(BF16), v_h)
                m_sc[h, rows, :] = m_new

    @pl.when(ki < qi)
    def _():
        step(False)

    @pl.when(ki == qi)
    def _():
        step(True)
        for h in range(N_HEADS):
            on = acc_sc[h] / jnp.sum(l_sc[h], axis=1, keepdims=True)
            y = _diff_finish(on[:T], on[T:], lam_ref[...], gain_ref[:, heads[h]], out_scale)
            o_ref[:, heads[h]] = y.astype(o_ref.dtype)


def _attn_prompt(q, k, v, lam_row, gain, out_scale):
    B, L, _ = q.shape
    T = ATT_TILE if L % ATT_TILE == 0 else SEQ_TILE
    n = L // T
    pairs = [(a, b) for a in range(n) for b in range(a + 1)]
    qi_arr = jnp.asarray([a for a, _ in pairs], jnp.int32)
    ki_arr = jnp.asarray([b for _, b in pairs], jnp.int32)
    q_spec = pl.BlockSpec((None, T, ATT_ALL), lambda b, p, qi, ki: (b, qi[p], 0))
    kv_spec = pl.BlockSpec((None, T, ATT_ALL), lambda b, p, qi, ki: (b, ki[p], 0))
    grid_spec = pltpu.PrefetchScalarGridSpec(
        num_scalar_prefetch=2,
        grid=(B, len(pairs)),
        in_specs=[q_spec, kv_spec, kv_spec,
                  pl.BlockSpec((1, ATT_W), lambda b, p, qi, ki: (0, 0)),
                  pl.BlockSpec((1, ATT_ALL), lambda b, p, qi, ki: (0, 0))],
        out_specs=q_spec,
        scratch_shapes=[pltpu.VMEM((N_HEADS, 2 * T, ATT_W), BF16), pltpu.VMEM((N_HEADS, 2 * T, ATT_W), F32),
                        pltpu.VMEM((N_HEADS, 2 * T, ATT_W), F32), pltpu.VMEM((N_HEADS, 2 * T, ATT_W), F32)],
    )
    return pl.pallas_call(
        functools.partial(_attn_kernel, T, out_scale),
        grid_spec=grid_spec,
        out_shape=jax.ShapeDtypeStruct((B, L, ATT_ALL), BF16),
        compiler_params=_params("parallel", "arbitrary"),
        name="attn_prompt",
    )(qi_arr, ki_arr, q, k, v, lam_row, gain)


def _dec_attn_kernel(npg, ls, page, layer, out_scale, pt_ref, q_ref, kn_ref, vn_ref, lam_ref, gain_ref,
                     ck_hbm, cv_hbm, o_ref, kbuf, vbuf, ksem, vsem, qs_sc, m_sc, l_sc, acc_sc):
    b = pl.program_id(0)
    j = pl.program_id(1)
    n_steps = pl.num_programs(1)
    step = b * n_steps + j
    slot = step % 2
    rows = 2 * N_HEADS * ls
    cols = page * N_HEADS
    row_head = lax.broadcasted_iota(jnp.int32, (rows, 1), 0) // (2 * ls)

    def fetch(seq, blk, buf):
        for i in range(npg):
            pg = pt_ref[seq, blk * npg + i]
            pltpu.make_async_copy(ck_hbm.at[layer, pg], kbuf.at[buf, i], ksem.at[buf]).start()
            pltpu.make_async_copy(cv_hbm.at[layer, pg], vbuf.at[buf, i], vsem.at[buf]).start()

    @pl.when(step == 0)
    def _():
        fetch(0, 0, 0)

    @pl.when(step + 1 < pl.num_programs(0) * n_steps)
    def _():
        wrap = j + 1 == n_steps
        fetch(jnp.where(wrap, b + 1, b), jnp.where(wrap, 0, j + 1), 1 - slot)

    @pl.when(j == 0)
    def _():
        q = q_ref[...].astype(F32)
        lane = lax.broadcasted_iota(jnp.int32, (1, ATT_W), 1)
        parts = []
        for h in range(N_HEADS):
            qh = q[:, h * ATT_W:(h + 1) * ATT_W]
            parts += [jnp.where(lane < HEAD_W, qh, 0.0), jnp.where(lane >= HEAD_W, qh, 0.0)]
        qs_sc[...] = jnp.concatenate(parts, axis=0).astype(BF16)
        m_sc[...] = jnp.full(m_sc.shape, NEG, F32)
        l_sc[...] = jnp.zeros(l_sc.shape, F32)
        acc_sc[...] = jnp.zeros(acc_sc.shape, F32)

    qs = qs_sc[...]

    def update(scores, values):
        m_old = m_sc[...]
        m_new = m_old
        for s in scores:
            m_new = jnp.maximum(m_new, jnp.max(s, axis=1, keepdims=True))
        alpha = jnp.exp2(m_old - m_new)
        l_new = alpha * l_sc[...]
        acc = alpha * acc_sc[...]
        for s, val in zip(scores, values):
            p = jnp.exp2(s - m_new)
            l_new = l_new + jnp.sum(p, axis=1, keepdims=True)
            acc = acc + _dot(p.astype(BF16), val)
        l_sc[...] = l_new
        acc_sc[...] = acc
        m_sc[...] = m_new

    pltpu.make_async_copy(ck_hbm.at[layer, pl.ds(0, npg)], kbuf.at[slot], ksem.at[slot]).wait()
    pltpu.make_async_copy(cv_hbm.at[layer, pl.ds(0, npg)], vbuf.at[slot], vsem.at[slot]).wait()
    own_head = (lax.broadcasted_iota(jnp.int32, (1, cols), 1) % N_HEADS) == row_head
    update([jnp.where(own_head, _nt(qs, kbuf[slot, i].astype(BF16)), NEG) for i in range(npg)],
           [vbuf[slot, i].astype(BF16) for i in range(npg)])

    @pl.when(j == n_steps - 1)
    def _():
        n_new = ls * N_HEADS
        kn = _pad_rows(kn_ref[...], ATT_W).astype(BF16)
        vn = _pad_rows(vn_ref[...], ATT_W).astype(BF16)
        col = lax.broadcasted_iota(jnp.int32, (1, ATT_W), 1)
        q_idx = lax.broadcasted_iota(jnp.int32, (rows, 1), 0) % ls
        ok = (col % N_HEADS == row_head) & (col // N_HEADS <= q_idx) & (col < n_new)
        update([jnp.where(ok, _nt(qs, kn), NEG)], [vn])
        on = acc_sc[...] / l_sc[...]
        gain = gain_ref[...]
        parts = []
        for h in range(N_HEADS):
            r0 = h * 2 * ls
            parts.append(_diff_finish(on[r0:r0 + ls], on[r0 + ls:r0 + 2 * ls], lam_ref[...],
                                      gain[:, h * ATT_W:(h + 1) * ATT_W], out_scale))
        o_ref[...] = jnp.concatenate(parts, axis=1).astype(o_ref.dtype)


def _attn_sample(q, k_new, v_new, cache_k, cache_v, page_table, layer, lam_row, gain, out_scale):
    B, ls, _ = q.shape
    n_pages = page_table.shape[1]
    page = cache_k.shape[2]
    cols = page * N_HEADS
    npg = next(n for n in (PAGES_PER_STEP, 8, 4, 2, 1) if n_pages % n == 0)
    ck = cache_k.reshape(cache_k.shape[0], cache_k.shape[1], cols, ATT_W)
    cv = cache_v.reshape(cache_v.shape[0], cache_v.shape[1], cols, ATT_W)
    kn = k_new.reshape(B, ls * N_HEADS, ATT_W)
    vn = v_new.reshape(B, ls * N_HEADS, ATT_W)
    per_b = pl.BlockSpec((None, ls, ATT_ALL), lambda b, j, pt: (b, 0, 0))
    new_spec = pl.BlockSpec((None, ls * N_HEADS, ATT_W), lambda b, j, pt: (b, 0, 0))
    rows = 2 * N_HEADS * ls
    grid_spec = pltpu.PrefetchScalarGridSpec(
        num_scalar_prefetch=1,
        grid=(B, n_pages // npg),
        in_specs=[per_b, new_spec, new_spec,
                  pl.BlockSpec((1, ATT_W), lambda b, j, pt: (0, 0)),
                  pl.BlockSpec((1, ATT_ALL), lambda b, j, pt: (0, 0)),
                  pl.BlockSpec(memory_space=pl.ANY), pl.BlockSpec(memory_space=pl.ANY)],
        out_specs=per_b,
        scratch_shapes=[pltpu.VMEM((2, npg, cols, ATT_W), F32), pltpu.VMEM((2, npg, cols, ATT_W), F32),
                        pltpu.SemaphoreType.DMA((2,)), pltpu.SemaphoreType.DMA((2,)),
                        pltpu.VMEM((rows, ATT_W), BF16), pltpu.VMEM((rows, 1), F32), pltpu.VMEM((rows, 1), F32),
                        pltpu.VMEM((rows, ATT_W), F32)],
    )
    return pl.pallas_call(
        functools.partial(_dec_attn_kernel, npg, ls, page, layer, out_scale),
        grid_spec=grid_spec,
        out_shape=jax.ShapeDtypeStruct((B, ls, ATT_ALL), BF16),
        compiler_params=_params("arbitrary", "arbitrary"),
        name="attn_sample",
    )(page_table, q, kn, vn, lam_row, gain, ck, cv)


def _outproj_kernel(alpha, ma_ref, mb_ref, mc_ref, h_ref, wo_ref, g_ref, b_ref, wr_ref, br_ref,
                    h1_ref, h1t_ref, eid_ref, gate_ref):
    W = REC_W
    mix = (_dot(ma_ref[...], wo_ref[0:W, :]) + _dot(mb_ref[...], wo_ref[W:W + ATT_ALL, :])
           + _dot(mc_ref[...], wo_ref[W + ATT_ALL:2 * W + ATT_ALL, :]))
    h1 = _layernorm(alpha * h_ref[...] + mix, g_ref[...], b_ref[...])
    h1_ref[...] = h1
    _to_row_tiles(h1t_ref, h1)

    logits = _dot(h1.astype(BF16), wr_ref[...]) + br_ref[...]
    lane = lax.broadcasted_iota(jnp.int32, (1, ROUTE_LANES), 1)
    lane_f = lane.astype(F32)
    far = float(ROUTE_LANES)

    gl = jnp.where(lane < N_GROUPS, logits, NEG)
    g_max = jnp.max(gl, axis=1, keepdims=True)
    g_top = 1.0 / jnp.sum(jnp.exp(gl - g_max), axis=1, keepdims=True)
    g_idx = jnp.min(jnp.where(gl == g_max, lane_f, far), axis=1, keepdims=True)

    e_lane = lane_f - float(EXPERT_LANE0)
    in_group = (e_lane >= g_idx * GROUP_EXPERTS) & (e_lane < (g_idx + 1.0) * GROUP_EXPERTS)
    el = jnp.where(in_group, logits, NEG)
    e_max = jnp.max(el, axis=1, keepdims=True)
    e_sum = jnp.sum(jnp.exp(el - e_max), axis=1, keepdims=True)
    idx1 = jnp.min(jnp.where(el == e_max, lane_f, far), axis=1, keepdims=True)
    el2 = jnp.where(lane_f == idx1, NEG, el)
    e_max2 = jnp.max(el2, axis=1, keepdims=True)
    idx2 = jnp.min(jnp.where(el2 == e_max2, lane_f, far), axis=1, keepdims=True)
    p1 = 1.0 / e_sum
    p2 = jnp.exp(e_max2 - e_max) / e_sum
    tot = p1 + p2
    gate_ref[...] = jnp.where(lane == 0, g_top * (p1 / tot), jnp.where(lane == 1, g_top * (p2 / tot), 0.0))
    eid = jnp.where(lane == 0, idx1 - float(EXPERT_LANE0), jnp.where(lane == 1, idx2 - float(EXPERT_LANE0), 0.0))
    eid_ref[...] = eid.astype(jnp.int32)


def _outproj(ma, mb, mc, h, wo, g, b, wr, br, alpha):
    T, D = h.shape
    tm = min(ROW_TILE, T)
    tile = lambda n: pl.BlockSpec((tm, n), lambda i: (i, 0))
    const = lambda r, n: pl.BlockSpec((r, n), lambda i: (0, 0))
    return pl.pallas_call(
        functools.partial(_outproj_kernel, alpha),
        grid=(T // tm,),
        in_specs=[tile(REC_W), tile(ATT_ALL), tile(REC_W), tile(D), const(2 * REC_W + ATT_ALL, D),
                  const(1, D), const(1, D), const(D, ROUTE_LANES), const(1, ROUTE_LANES)],
        out_specs=[tile(D), pl.BlockSpec((tm, D // LANES, LANES), lambda i: (i, 0, 0)), tile(ROUTE_LANES),
                   tile(ROUTE_LANES)],
        out_shape=[jax.ShapeDtypeStruct((T, D), F32), jax.ShapeDtypeStruct((T, D // LANES, LANES), F32),
                   jax.ShapeDtypeStruct((T, ROUTE_LANES), jnp.int32),
                   jax.ShapeDtypeStruct((T, ROUTE_LANES), F32)],
        compiler_params=_params("parallel"),
        name="outproj",
    )(ma, mb, mc, h, wo, g, b, wr, br)


def _expert_kernel(row_ref, blke_ref, nval_ref, x_hbm, wg_ref, wu_ref, wd_ref, out_hbm,
                   xbuf, ybuf, gsem, ssem):
    del blke_ref
    i = pl.program_id(0)
    nblk = pl.num_programs(0)
    cur = i % 2

    def start_gather(blk, buf):
        @pl.when(nval_ref[blk] > 0)
        def _():
            def body(r, carry):
                tok = lax.shift_right_logical(row_ref[blk * MOE_ROWS + r], 1)
                pltpu.make_async_copy(x_hbm.at[pl.ds(tok, 1)], xbuf.at[buf, pl.ds(r, 1)], gsem.at[buf]).start()
                return carry
            lax.fori_loop(0, MOE_ROWS, body, 0, unroll=SUB)

    def start_scatter(blk, buf):
        def one(r):
            code = row_ref[blk * MOE_ROWS + r]
            pltpu.make_async_copy(ybuf.at[buf, pl.ds(r, 1)],
                                  out_hbm.at[code & 1, pl.ds(lax.shift_right_logical(code, 1), 1)],
                                  ssem.at[buf]).start()

        n = nval_ref[blk]
        full = lax.shift_right_logical(n, 3)

        def group(g, carry):
            for u in range(SUB):
                one(g * SUB + u)
            return carry
        lax.fori_loop(0, full, group, 0)

        def rest(r, carry):
            one(r)
            return carry
        lax.fori_loop(full * SUB, n, rest, 0)

    def wait_scatter(blk, buf):
        n = nval_ref[blk]

        @pl.when(n > 0)
        def _():
            pltpu.make_async_copy(ybuf.at[buf, pl.ds(0, n)], out_hbm.at[0, pl.ds(0, n)], ssem.at[buf]).wait()

    @pl.when(i == 0)
    def _():
        start_gather(0, 0)

    @pl.when(i + 1 < nblk)
    def _():
        start_gather(i + 1, 1 - cur)

    @pl.when(i >= 2)
    def _():
        wait_scatter(i - 2, cur)

    @pl.when(nval_ref[i] > 0)
    def _():
        pltpu.make_async_copy(x_hbm.at[pl.ds(0, MOE_ROWS)], xbuf.at[cur], gsem.at[cur]).wait()
        xb = _from_row_tiles(xbuf.at[cur]).astype(BF16)
        hid = _silu(_dot(xb, wg_ref[...])) * _dot(xb, wu_ref[...])
        _to_row_tiles(ybuf.at[cur], _dot(hid.astype(BF16), wd_ref[...]))
        start_scatter(i, cur)

    @pl.when(i == nblk - 1)
    def _():
        @pl.when(i >= 1)
        def _():
            wait_scatter(i - 1, 1 - cur)
        wait_scatter(i, cur)


def _experts(x_tiles, row_code, blk_e, blk_nval, wg, wu, wd):
    T, chunks, _ = x_tiles.shape
    D = chunks * LANES
    n_blocks = blk_e.shape[0]
    de = wg.shape[2]
    grid_spec = pltpu.PrefetchScalarGridSpec(
        num_scalar_prefetch=3,
        grid=(n_blocks,),
        in_specs=[pl.BlockSpec(memory_space=pl.ANY),
                  pl.BlockSpec((None, D, de), lambda i, code, be, nv: (be[i], 0, 0)),
                  pl.BlockSpec((None, D, de), lambda i, code, be, nv: (be[i], 0, 0)),
                  pl.BlockSpec((None, de, D), lambda i, code, be, nv: (be[i], 0, 0))],
        out_specs=pl.BlockSpec(memory_space=pl.ANY),
        scratch_shapes=[pltpu.VMEM((2, MOE_ROWS, chunks, LANES), F32), pltpu.VMEM((2, MOE_ROWS, chunks, LANES), F32),
                        pltpu.SemaphoreType.DMA((2,)), pltpu.SemaphoreType.DMA((2,))],
    )
    return pl.pallas_call(
        _expert_kernel,
        grid_spec=grid_spec,
        out_shape=jax.ShapeDtypeStruct((2, T, chunks, LANES), F32),
        compiler_params=_params("arbitrary"),
        name="experts",
    )(row_code, blk_e, blk_nval, x_tiles, wg, wu, wd)


def _dispatch(eid, T):
    K = 2
    A = T * K
    flat_e = eid[:, :K].reshape(-1)
    onehot = (flat_e[:, None] == jnp.arange(N_EXPERTS, dtype=jnp.int32)[None, :]).astype(F32)
    a_pad = -(-A // MOE_ROWS) * MOE_ROWS
    oh3 = jnp.pad(onehot, ((0, a_pad - A), (0, 0))).reshape(a_pad // MOE_ROWS, MOE_ROWS, N_EXPERTS)
    tril = jnp.tril(jnp.ones((MOE_ROWS, MOE_ROWS), F32))
    within = jnp.einsum('ts,bse->bte', tril, oh3, precision=HIGHEST)
    blk_tot = within[:, -1, :]
    before = jnp.cumsum(blk_tot, axis=0) - blk_tot
    rank = jnp.sum((within + before[:, None, :]) * oh3, axis=-1).reshape(a_pad)[:A] - 1.0
    counts = jnp.sum(blk_tot, axis=0).astype(jnp.int32)
    padded = (counts + MOE_ROWS - 1) // MOE_ROWS * MOE_ROWS
    pad_end = jnp.cumsum(padded)
    pad_start = pad_end - padded
    dest = jnp.sum(onehot * pad_start.astype(F32)[None, :], axis=-1) + rank
    n_blocks = -(-A // MOE_ROWS) + N_EXPERTS
    R = n_blocks * MOE_ROWS
    row_code = jnp.zeros((R,), jnp.int32).at[dest.astype(jnp.int32)].set(jnp.arange(A, dtype=jnp.int32))
    blk_start = jnp.arange(n_blocks, dtype=jnp.int32) * MOE_ROWS
    blk_e = jnp.minimum(jnp.sum((pad_end[None, :] <= blk_start[:, None]).astype(jnp.int32), axis=1), N_EXPERTS - 1)
    blk_nval = jnp.clip((pad_start + counts)[blk_e] - blk_start, 0, MOE_ROWS).astype(jnp.int32)
    return row_code, blk_e.astype(jnp.int32), blk_nval


def _combine_kernel(alpha, h_ref, y_ref, gate_ref, g_ref, b_ref, o_ref):
    gate = gate_ref[...]
    y = gate[:, 0:1] * _from_row_tiles(y_ref.at[0]) + gate[:, 1:2] * _from_row_tiles(y_ref.at[1])
    o_ref[...] = _layernorm(alpha * h_ref[...] + y, g_ref[...], b_ref[...])


def _combine(h1, slots, gates, g, b, alpha):
    T, D = h1.shape
    tm = min(ROW_TILE, T)
    return pl.pallas_call(
        functools.partial(_combine_kernel, alpha),
        grid=(T // tm,),
        in_specs=[pl.BlockSpec((tm, D), lambda i: (i, 0)), pl.BlockSpec((2, tm, D // LANES, LANES), lambda i: (0, i, 0, 0)),
                  pl.BlockSpec((tm, ROUTE_LANES), lambda i: (i, 0)),
                  pl.BlockSpec((1, D), lambda i: (0, 0)), pl.BlockSpec((1, D), lambda i: (0, 0))],
        out_specs=pl.BlockSpec((tm, D), lambda i: (i, 0)),
        out_shape=jax.ShapeDtypeStruct((T, D), F32),
        compiler_params=_params("parallel"),
        name="combine",
    )(h1, slots, gates, g, b)


def _block_diag(blocks):
    B, H, n, _ = blocks.shape
    eye = jnp.eye(H, dtype=blocks.dtype)
    return jnp.einsum('bhij,hg->bhigj', blocks, eye).reshape(B, H * n, H * n)


def _diag_blocks(mat, H):
    B, n = mat.shape[0], mat.shape[1] // H
    m5 = mat.reshape(B, H, n, H, n)
    return jnp.stack([m5[:, h, :, h, :] for h in range(H)], axis=1)


def _layer(x, do_ln, B, L, l_real, lw, state, past):
    n_seq, l_out = (B, l_real) if past is None else (1, B * L)
    outs = _inproj(x, lw['ln_g'], lw['ln_b'], lw['w_in'], lw['b_in'], do_ln, n_seq, l_out)
    a, qb, kb, vb, cg, gates, kb16, vb16 = outs[:8]
    h = outs[8] if do_ln else x
    s3 = lambda z: z.reshape(B, L, z.shape[-1])

    mix_a, st_a = _hgrn(s3(a), lw['lb'], lw['norm_a'], state['hgrn'], l_real)
    mix_c, ct, n_c, m_c = _mlstm(s3(cg), s3(gates), lw['conv_w'], lw['conv_b'], lw['b_f'], lw['norm_c'],
                                 state['c'], state['n'], state['m'], state['tail'], l_real)
    if past is None:
        mix_b = _attn_prompt(s3(qb), s3(kb16), s3(vb16), lw['lam'], lw['norm_b'], lw['out_scale'])
    else:
        mix_b = _attn_sample(s3(qb), kb, vb, past[0], past[1], past[2], lw['layer'], lw['lam'],
                             lw['norm_b'], lw['out_scale'])

    flat = lambda z: z.reshape(B * L, z.shape[-1])
    h1, h1_tiles, eid, gate = _outproj(flat(mix_a), flat(mix_b), flat(mix_c), h, lw['w_out'], lw['ln1_g'], lw['ln1_b'],
                             lw['w_route'], lw['b_route'], lw['alpha'])
    row_code, blk_e, blk_nval = _dispatch(eid, B * L)
    slots = _experts(h1_tiles, row_code, blk_e, blk_nval, lw['w_eg'], lw['w_eu'], lw['w_ed'])
    h2 = _combine(h1, slots, gate, lw['ln2_g'], lw['ln2_b'], lw['alpha'])

    new_state = {
        'k': kb.reshape(B, l_real, N_HEADS, ATT_W), 'v': vb.reshape(B, l_real, N_HEADS, ATT_W),
        'hgrn': jnp.swapaxes(_diag_blocks(st_a, N_HEADS), -1, -2),
        'c': _diag_blocks(ct, N_HEADS),
        'n': n_c.reshape(B, N_HEADS, HEAD_W),
        'm': m_c.reshape(B, N_HEADS, HEAD_W)[:, :, 0],
        'conv': s3(cg)[:, l_real - (CONV_TAPS - 1):l_real, :2 * REC_W],
    }
    return h2, new_state


def kernel(x_prompt, x_sample, cache_k, cache_v, page_table, state_hgrn, state_mlstm_c, state_mlstm_n, state_mlstm_m, state_mlstm_conv, meta_tokens, ln_emb_g, ln_emb_b, w_in, b_in, b_mlstm_f, conv_w, conv_b, hgrn_lb_logits, lambda_q1, lambda_k1, lambda_q2, lambda_k2, norm_a, norm_b, norm_c, w_out, ln1_g, ln1_b, w_router_group, b_router_group, w_router_expert, b_router_expert, w_exp_gate, w_exp_up, w_exp_down, ln2_g, ln2_b):
    depth = w_in.shape[0]
    Bp, Lq, D = x_prompt.shape
    Bs, Ls, _ = x_sample.shape
    n_meta = meta_tokens.shape[0]
    l_real = n_meta + Lq
    Lp = -(-l_real // SEQ_TILE) * SEQ_TILE
    alpha = (2.0 * depth) ** 0.25
    row = lambda z: z.reshape(1, -1).astype(F32)

    meta = jnp.broadcast_to(meta_tokens[None].astype(x_prompt.dtype), (Bp, n_meta, D))
    xp = jnp.concatenate([meta, x_prompt, jnp.zeros((Bp, Lp - l_real, D), x_prompt.dtype)], axis=1)
    hp = xp.reshape(Bp * Lp, D)
    hs = x_sample.reshape(Bs * Ls, D)

    lb_sm = jax.nn.softmax(hgrn_lb_logits.astype(F32), axis=0)
    lb_all = jnp.cumsum(lb_sm, axis=0) - lb_sm[0]

    zeros_p = {
        'hgrn': jnp.zeros((Bp, REC_W, REC_W), F32), 'c': jnp.zeros((Bp, REC_W, REC_W), F32),
        'n': jnp.zeros((Bp, 1, REC_W), F32), 'm': jnp.zeros((Bp, 1, REC_W), F32),
        'tail': jnp.zeros((Bp, SUB, 2 * REC_W), F32),
    }

    res_p, res_s = [], []
    for l in range(depth):
        lam_init = 0.8 - 0.6 * math.exp(-0.3 * l)
        lam = (jnp.exp(jnp.sum(lambda_q1[l].astype(F32) * lambda_k1[l].astype(F32)))
               - jnp.exp(jnp.sum(lambda_q2[l].astype(F32) * lambda_k2[l].astype(F32))) + lam_init)
        pad_cols = N_PROJ_PAD - N_PROJ_RAW
        bias_f = jnp.zeros((ROUTE_LANES,), F32).at[N_HEADS:2 * N_HEADS].set(b_mlstm_f[l].astype(F32))
        w_route = jnp.concatenate([w_router_group[l], w_router_expert[l]], axis=1)
        b_route = jnp.concatenate([b_router_group[l], b_router_expert[l]])
        r_pad = ROUTE_LANES - w_route.shape[1]
        lw = {
            'layer': l, 'alpha': alpha, 'out_scale': 1.0 - lam_init,
            'ln_g': row(ln_emb_g), 'ln_b': row(ln_emb_b),
            'w_in': jnp.pad(w_in[l], ((0, 0), (0, pad_cols))).astype(BF16),
            'b_in': row(jnp.pad(b_in[l], (0, pad_cols))),
            'lb': row(lb_all[l]), 'norm_a': row(norm_a[l]), 'norm_b': row(norm_b[l]), 'norm_c': row(norm_c[l]),
            'lam': jnp.full((1, ATT_W), lam, F32),
            'conv_w': conv_w[l].astype(F32), 'conv_b': row(conv_b[l]), 'b_f': row(bias_f),
            'w_out': w_out[l].astype(BF16), 'ln1_g': row(ln1_g[l]), 'ln1_b': row(ln1_b[l]),
            'w_route': jnp.pad(w_route, ((0, 0), (0, r_pad))).astype(BF16), 'b_route': row(jnp.pad(b_route, (0, r_pad))),
            'w_eg': w_exp_gate[l].astype(BF16), 'w_eu': w_exp_up[l].astype(BF16), 'w_ed': w_exp_down[l].astype(BF16),
            'ln2_g': row(ln2_g[l]), 'ln2_b': row(ln2_b[l]),
        }
        hp, st_p = _layer(hp, l == 0, Bp, Lp, l_real, lw, zeros_p, None)

        state_s = {
            'hgrn': _block_diag(jnp.swapaxes(state_hgrn[l].astype(F32), -1, -2)),
            'c': _block_diag(state_mlstm_c[l].astype(F32)),
            'n': state_mlstm_n[l].astype(F32).reshape(Bs, 1, REC_W),
            'm': jnp.repeat(state_mlstm_m[l].astype(F32), HEAD_W, axis=-1).reshape(Bs, 1, REC_W),
            'tail': jnp.pad(state_mlstm_conv[l].astype(F32), ((0, 0), (SUB - (CONV_TAPS - 1), 0), (0, 0))),
        }
        hs, st_s = _layer(hs, l == 0, Bs, Ls, Ls, lw, state_s, (cache_k, cache_v, page_table))
        res_p.append(st_p)
        res_s.append(st_s)

    y_prompt = hp.reshape(Bp, Lp, D)[:, n_meta:l_real]
    y_sample = hs.reshape(Bs, Ls, D)
    stack = lambda res, key, f=lambda z: z: jnp.stack([f(r[key]) for r in res])
    return (y_prompt, y_sample,
            stack(res_p, 'k'), stack(res_p, 'v'), stack(res_s, 'k'), stack(res_s, 'v'),
            stack(res_p, 'hgrn'), stack(res_s, 'hgrn'),
            stack(res_p, 'c'), stack(res_s, 'c'),
            stack(res_p, 'n'), stack(res_s, 'n'),
            stack(res_p, 'm'), stack(res_s, 'm'),
            stack(res_p, 'conv'), stack(res_s, 'conv'))
```

```python
import functools
import math

import jax
import jax.numpy as jnp
from jax import lax
from jax.experimental import pallas as pl
from jax.experimental.pallas import tpu as pltpu

F32 = jnp.float32
BF16 = jnp.bfloat16
HIGHEST = lax.Precision.HIGHEST

N_META = 16
N_HEADS = 4
HEAD_W = 64
ATT_W = 2 * HEAD_W
REC_W = N_HEADS * HEAD_W
ATT_ALL = N_HEADS * ATT_W
CONV_TAPS = 4
N_GROUPS = 4
GROUP_EXPERTS = 8
N_EXPERTS = N_GROUPS * GROUP_EXPERTS
LANES = 128
ROUTE_LANES = LANES
EXPERT_LANE0 = N_GROUPS
MOE_ROWS = 256
LN_EPS = 1e-5
NORM_EPS = 1e-6
NEG = -1e30
LOG2_E = 1.4426950408889634

SEQ_TILE = 128
REC_CHUNK = 128
SUB = 8
ATT_TILE = 384
ATT_ROWS = 768
PAGES_PER_STEP = 16
ROW_TILE = 512
INPROJ_TILES = (384, 256, 128)
VMEM_LIMIT = 48 * 1024 * 1024

IN_WIDTHS = (4 * REC_W, ATT_ALL, ATT_ALL, ATT_ALL, 4 * REC_W, ROUTE_LANES)
N_PROJ_RAW = 4 * REC_W + 3 * ATT_ALL + 4 * REC_W + 2 * N_HEADS
N_PROJ_PAD = sum(IN_WIDTHS)


def _nt(a, b):
    return lax.dot_general(a, b, (((1,), (1,)), ((), ())), preferred_element_type=F32)


def _tn(a, b):
    return lax.dot_general(a, b, (((0,), (0,)), ((), ())), preferred_element_type=F32)


def _dot(a, b, precision=None):
    return jnp.dot(a, b, preferred_element_type=F32, precision=precision)


def _layernorm(x, g, b):
    mu = jnp.mean(x, axis=-1, keepdims=True)
    xc = x - mu
    var = jnp.mean(xc * xc, axis=-1, keepdims=True)
    return xc * lax.rsqrt(var + LN_EPS) * g + b


def _sigmoid(x):
    return 1.0 / (1.0 + jnp.exp(-x))


def _silu(x):
    return x * _sigmoid(x)


def _head_lane_masks(width):
    lane = lax.broadcasted_iota(jnp.int32, (1, width), 1)
    per = width // N_HEADS
    return [(lane // per == h).astype(F32) for h in range(N_HEADS)]


def _block_diag_mask(n, blk):
    r = lax.broadcasted_iota(jnp.int32, (n, n), 0) // blk
    c = lax.broadcasted_iota(jnp.int32, (n, n), 1) // blk
    return r == c


def _prefix_sum_rows(x):
    n = x.shape[0]
    r = lax.broadcasted_iota(jnp.int32, (n, n), 0)
    c = lax.broadcasted_iota(jnp.int32, (n, n), 1)
    tril = (r >= c).astype(BF16)
    hi = x.astype(BF16)
    rest = x - hi.astype(F32)
    mid = rest.astype(BF16)
    lo = (rest - mid.astype(F32)).astype(BF16)
    return _dot(tril, hi) + (_dot(tril, mid) + _dot(tril, lo))


def _to_row_tiles(ref_at, x):
    for c in range(x.shape[1] // LANES):
        ref_at[:, c, :] = x[:, c * LANES:(c + 1) * LANES]


def _from_row_tiles(ref_at):
    return jnp.concatenate([ref_at[:, c, :] for c in range(ref_at.shape[1])], axis=1)


def _params(*sem):
    return pltpu.CompilerParams(dimension_semantics=sem, vmem_limit_bytes=VMEM_LIMIT)


def _inproj_kernel(do_ln, x_ref, g_ref, b_ref, w_ref, bias_ref, *outs):
    x = x_ref[...]
    if do_ln:
        x = _layernorm(x, g_ref[...], b_ref[...])
        outs[-1][...] = x
    xb = x.astype(BF16)
    col = 0
    for idx, width in enumerate(IN_WIDTHS):
        acc = _dot(xb, w_ref[:, col:col + width]) + bias_ref[:, col:col + width]
        if idx == 1:
            acc = acc * (HEAD_W ** -0.5 * LOG2_E)
        if idx in (2, 3):
            _to_row_tiles(outs[idx], acc)
            outs[len(IN_WIDTHS) + idx - 2][...] = acc.astype(BF16)
        else:
            outs[idx][...] = acc.astype(outs[idx].dtype)
        col += width


def _inproj(x, ln_g, ln_b, w, bias, do_ln, n_seq, l_out):
    T, D = x.shape
    L = T // n_seq
    tm = next((t for t in INPROJ_TILES if L % t == 0), L)
    nj = L // tm
    widths = IN_WIDTHS + (ATT_ALL, ATT_ALL)
    dtypes = (F32, BF16, F32, F32, F32, F32, BF16, BF16)
    out_shape = [jax.ShapeDtypeStruct((T, wd), dt) for wd, dt in zip(widths, dtypes)]
    out_specs = [pl.BlockSpec((tm, wd), lambda i: (i, 0)) for wd in widths]
    for idx in (2, 3):
        out_shape[idx] = jax.ShapeDtypeStruct((n_seq, l_out, N_HEADS, ATT_W), F32)
        out_specs[idx] = pl.BlockSpec((None, tm, N_HEADS, ATT_W), lambda i: (i // nj, i % nj, 0, 0))
    if do_ln:
        out_shape.append(jax.ShapeDtypeStruct((T, D), F32))
        out_specs.append(pl.BlockSpec((tm, D), lambda i: (i, 0)))
    row = lambda n: pl.BlockSpec((1, n), lambda i: (0, 0))
    return pl.pallas_call(
        functools.partial(_inproj_kernel, do_ln),
        grid=(T // tm,),
        in_specs=[pl.BlockSpec((tm, D), lambda i: (i, 0)), row(D), row(D),
                  pl.BlockSpec((D, N_PROJ_PAD), lambda i: (0, 0)), row(N_PROJ_PAD)],
        out_specs=out_specs,
        out_shape=out_shape,
        compiler_params=_params("parallel"),
        name="inproj",
    )(x, ln_g, ln_b, w, bias)


def _pad_rows(a, rows):
    if a.shape[0] == rows:
        return a
    return jnp.concatenate([a, jnp.zeros((rows - a.shape[0], a.shape[1]), a.dtype)], axis=0)


def _hgrn_kernel(C, Cb, l_real, a_ref, lb_ref, gain_ref, s0_ref, o_ref, sout_ref, st_ref):
    c = pl.program_id(1)

    @pl.when(c == 0)
    def _():
        st_ref[...] = s0_ref[...]

    W = REC_W
    a = _pad_rows(a_ref[...], C)
    q, fa, v, ga = a[:, 0:W], a[:, W:2 * W], a[:, 2 * W:3 * W], a[:, 3 * W:4 * W]
    lb = lb_ref[...]
    valid = (c * C + lax.broadcasted_iota(jnp.int32, (C, 1), 0)) < l_real
    f = lb + (1.0 - lb) * _sigmoid(fa)
    g = jnp.where(valid, jnp.log(f), 0.0)
    kk = jnp.where(valid, 1.0 - f, 0.0)
    b = _prefix_sum_rows(g)

    bd = _block_diag_mask(W, HEAD_W)
    bd_bf = bd.astype(BF16)
    hm = _head_lane_masks(W)

    nb = C // SUB
    b3, q3, k3, v3 = (z.reshape(nb, SUB, W) for z in (b, q, kk, v))
    t_in = lax.broadcasted_iota(jnp.int32, (nb, SUB, W), 1)
    o3 = jnp.zeros((nb, SUB, W), F32)
    for s in range(SUB):
        e = jnp.exp(jnp.minimum(b3 - b3[:, s:s + 1, :], 0.0))
        p = jnp.where(t_in >= s, e * q3 * k3[:, s:s + 1, :], 0.0)
        r = _dot(p.reshape(C, W).astype(BF16), bd_bf)
        o3 = o3 + r.reshape(nb, SUB, W) * v3[:, s:s + 1, :]
    o = o3.reshape(C, W)

    t_row = lax.broadcasted_iota(jnp.int32, (C, 1), 0)
    s_col = lax.broadcasted_iota(jnp.int32, (1, C), 1)
    att = [jnp.zeros((C, C), F32) for _ in range(N_HEADS)]
    m = SUB
    while m < C:
        blk = 2 * m
        ref = jnp.broadcast_to(b.reshape(C // blk, blk, W)[:, m - 1:m, :], (C // blk, blk, W)).reshape(C, W)
        later = (t_row % blk) >= m
        qt = jnp.where(later, q * jnp.exp(jnp.minimum(b - ref, 0.0)), 0.0)
        kt = jnp.where(later, 0.0, kk * jnp.exp(jnp.minimum(ref - b, 0.0))).astype(BF16)
        same = (t_row // blk) == (s_col // blk)
        for h in range(N_HEADS):
            sc = _nt((qt * hm[h]).astype(BF16), kt)
            att[h] = att[h] + jnp.where(same, sc, 0.0)
        m = blk
    if C > SUB:
        for h in range(N_HEADS):
            o = o + _dot(att[h].astype(BF16), (v * hm[h]).astype(BF16))

    st = st_ref[...]
    o = o + _nt((q * jnp.exp(b)).astype(BF16), st.astype(BF16))
    b_end = b[C - 1:C, :]
    ke = kk * jnp.exp(b_end - b)
    st_new = st * jnp.exp(b_end) + jnp.where(bd, _tn(v.astype(BF16), ke.astype(BF16)), 0.0)
    st_ref[...] = st_new
    sout_ref[...] = st_new

    ms = _dot((o * o).astype(BF16), bd_bf) * (1.0 / HEAD_W)
    y = o * lax.rsqrt(ms + NORM_EPS) * gain_ref[...] * _silu(ga)
    o_ref[...] = y[:Cb].astype(o_ref.dtype)


def _hgrn(a, lb, gain, s0, l_real):
    B, L, _ = a.shape
    Cb = min(REC_CHUNK, L)
    W = REC_W
    return pl.pallas_call(
        functools.partial(_hgrn_kernel, REC_CHUNK, Cb, l_real),
        grid=(B, L // Cb),
        in_specs=[pl.BlockSpec((None, Cb, 4 * W), lambda b, c: (b, c, 0)),
                  pl.BlockSpec((1, W), lambda b, c: (0, 0)),
                  pl.BlockSpec((1, W), lambda b, c: (0, 0)),
                  pl.BlockSpec((None, W, W), lambda b, c: (b, 0, 0))],
        out_specs=[pl.BlockSpec((None, Cb, W), lambda b, c: (b, c, 0)),
                   pl.BlockSpec((None, W, W), lambda b, c: (b, 0, 0))],
        out_shape=[jax.ShapeDtypeStruct((B, L, W), BF16), jax.ShapeDtypeStruct((B, W, W), F32)],
        scratch_shapes=[pltpu.VMEM((W, W), F32)],
        compiler_params=_params("parallel", "arbitrary"),
        name="hgrn",
    )(a, lb, gain, s0)


def _log_sigmoid(x):
    return jnp.minimum(x, 0.0) - jnp.log1p(jnp.exp(-jnp.abs(x)))


def _mlstm_kernel(C, Cb, l_real, cg_ref, gt_ref, cw_ref, cb_ref, bf_ref, gain_ref, c0_ref, n0_ref, m0_ref,
                  tail0_ref, o_ref, cout_ref, nout_ref, mout_ref, ct_ref, n_ref, m_ref, tail_ref):
    c = pl.program_id(1)

    @pl.when(c == 0)
    def _():
        ct_ref[...] = c0_ref[...]
        n_ref[...] = n0_ref[...]
        m_ref[...] = m0_ref[...]
        tail_ref[...] = tail0_ref[...]

    W = REC_W
    cg = _pad_rows(cg_ref[...], C)
    gt = _pad_rows(gt_ref[...], C)
    valid = (c * C + lax.broadcasted_iota(jnp.int32, (C, 1), 0)) < l_real

    qk_pre = cg[:, 0:2 * W]
    xp = jnp.concatenate([tail_ref[...], qk_pre], axis=0)
    conv = cb_ref[...]
    base = SUB - (CONV_TAPS - 1)
    for j in range(CONV_TAPS):
        conv = conv + cw_ref[j:j + 1, :] * xp[base + j:base + j + C, :]
    tail_ref[...] = qk_pre[C - SUB:C, :]
    qk = _silu(conv)
    qc = qk[:, 0:W]
    kc = qk[:, W:2 * W] * (HEAD_W ** -0.5)
    vc = cg[:, 2 * W:3 * W]
    oc = cg[:, 3 * W:4 * W]

    lf = jnp.where(valid, _log_sigmoid(gt + bf_ref[...]), 0.0)
    li = jnp.where(valid, gt, NEG)
    fcum = _prefix_sum_rows(lf)
    lane_r = lax.broadcasted_iota(jnp.int32, (ROUTE_LANES, W), 0)
    lane_c = lax.broadcasted_iota(jnp.int32, (ROUTE_LANES, W), 1) // HEAD_W
    li_b = _dot(li, (lane_r == lane_c).astype(F32), HIGHEST)
    f_b = _dot(fcum, (lane_r == lane_c + N_HEADS).astype(F32), HIGHEST)
    f_t = fcum.T
    li_t = li.T

    hm = _head_lane_masks(W)
    bd = _block_diag_mask(W, HEAD_W)
    causal = lax.broadcasted_iota(jnp.int32, (C, C), 0) >= lax.broadcasted_iota(jnp.int32, (C, C), 1)
    m_prev = m_ref[...]
    kc_bf = kc.astype(BF16)
    num = jnp.zeros((C, W), F32)
    den = jnp.zeros((C, W), F32)
    cs_b = jnp.zeros((C, W), F32)
    mt_b = jnp.zeros((C, W), F32)
    for h in range(N_HEADS):
        f_col = fcum[:, N_HEADS + h:N_HEADS + h + 1]
        log_d = jnp.where(causal, f_col - f_t[N_HEADS + h:N_HEADS + h + 1, :] + li_t[h:h + 1, :], NEG)
        log_s = f_col + m_prev[:, h * HEAD_W:h * HEAD_W + 1]
        m_t = jnp.maximum(jnp.max(log_d, axis=1, keepdims=True), log_s)
        w = _nt((qc * hm[h]).astype(BF16), kc_bf) * jnp.exp(log_d - m_t)
        num = num + _dot(w.astype(BF16), (vc * hm[h]).astype(BF16))
        den = den + jnp.sum(w, axis=1, keepdims=True) * hm[h]
        cs_b = cs_b + jnp.exp(log_s - m_t) * hm[h]
        mt_b = mt_b + m_t * hm[h]

    ct = ct_ref[...]
    n_row = n_ref[...]
    num = num + cs_b * _nt(qc.astype(BF16), ct.astype(BF16))
    den = den + cs_b * _dot((qc * n_row).astype(BF16), bd.astype(BF16))
    hh = num / jnp.maximum(jnp.abs(den), jnp.exp(-mt_b))

    m_new = mt_b[C - 1:C, :]
    f_end = f_b[C - 1:C, :]
    ws = jnp.exp(f_end - f_b + li_b - m_new)
    decay = jnp.exp(f_end + m_prev - m_new)
    ct_new = ct * decay + jnp.where(bd, _tn((vc * ws).astype(BF16), kc_bf), 0.0)
    n_new = decay * n_row + jnp.sum(ws * kc, axis=0, keepdims=True)
    ct_ref[...] = ct_new
    n_ref[...] = n_new
    m_ref[...] = m_new
    cout_ref[...] = ct_new
    nout_ref[...] = n_new
    mout_ref[...] = m_new

    ms = _dot((hh * hh).astype(BF16), bd.astype(BF16)) * (1.0 / HEAD_W)
    y = _sigmoid(oc) * (hh * lax.rsqrt(ms + NORM_EPS) * gain_ref[...])
    o_ref[...] = y[:Cb].astype(o_ref.dtype)


def _mlstm(cg, gates, conv_w, conv_b, bias_f, gain, c0, n0, m0, tail0, l_real):
    B, L, _ = cg.shape
    Cb = min(REC_CHUNK, L)
    W = REC_W
    const = lambda shape: pl.BlockSpec(shape, lambda b, c: (0,) * len(shape))
    per_b = lambda shape: pl.BlockSpec((None,) + shape, lambda b, c: (b,) + (0,) * len(shape))
    return pl.pallas_call(
        functools.partial(_mlstm_kernel, REC_CHUNK, Cb, l_real),
        grid=(B, L // Cb),
        in_specs=[pl.BlockSpec((None, Cb, 4 * W), lambda b, c: (b, c, 0)),
                  pl.BlockSpec((None, Cb, ROUTE_LANES), lambda b, c: (b, c, 0)),
                  const((CONV_TAPS, 2 * W)), const((1, 2 * W)), const((1, ROUTE_LANES)), const((1, W)),
                  per_b((W, W)), per_b((1, W)), per_b((1, W)), per_b((SUB, 2 * W))],
        out_specs=[pl.BlockSpec((None, Cb, W), lambda b, c: (b, c, 0)),
                   per_b((W, W)), per_b((1, W)), per_b((1, W))],
        out_shape=[jax.ShapeDtypeStruct((B, L, W), BF16), jax.ShapeDtypeStruct((B, W, W), F32),
                   jax.ShapeDtypeStruct((B, 1, W), F32), jax.ShapeDtypeStruct((B, 1, W), F32)],
        scratch_shapes=[pltpu.VMEM((W, W), F32), pltpu.VMEM((1, W), F32), pltpu.VMEM((1, W), F32),
                        pltpu.VMEM((SUB, 2 * W), F32)],
        compiler_params=_params("parallel", "arbitrary"),
        name="mlstm",
    )(cg, gates, conv_w, conv_b, bias_f, gain, c0, n0, m0, tail0)


def _diff_finish(o1, o2, lam, gain, out_scale):
    o = o1 - lam * o2
    ms = jnp.mean(o * o, axis=-1, keepdims=True)
    return out_scale * (o * lax.rsqrt(ms + NORM_EPS) * gain)


def _attn_kernel(T, out_scale, qi_ref, ki_ref, q_ref, k_ref, v_ref, lam_ref, gain_ref, o_ref,
                 qq_sc, m_sc, l_sc, acc_sc):
    step_id = pl.program_id(1)
    qi = qi_ref[step_id]
    ki = ki_ref[step_id]
    heads = [slice(h * ATT_W, (h + 1) * ATT_W) for h in range(N_HEADS)]
    rb = ATT_ROWS if (2 * T) % ATT_ROWS == 0 else T
    lane_tiles = T // ATT_W

    @pl.when(ki == 0)
    def _():
        lane = lax.broadcasted_iota(jnp.int32, (1, ATT_W), 1)
        for h in range(N_HEADS):
            q = q_ref[:, heads[h]]
            zero = jnp.zeros_like(q)
            qq_sc[h, 0:T, :] = jnp.where(lane < HEAD_W, q, zero)
            qq_sc[h, T:2 * T, :] = jnp.where(lane >= HEAD_W, q, zero)
        m_sc[...] = jnp.full(m_sc.shape, NEG, F32)
        l_sc[...] = jnp.zeros(l_sc.shape, F32)
        acc_sc[...] = jnp.zeros(acc_sc.shape, F32)

    def step(diagonal):
        for h in range(N_HEADS):
            k_h = k_ref[:, heads[h]]
            v_h = v_ref[:, heads[h]]
            for r0 in range(0, 2 * T, rb):
                rows = slice(r0, r0 + rb)
                s = _nt(qq_sc[h, rows, :], k_h)
                if diagonal:
                    row = (r0 + lax.broadcasted_iota(jnp.int32, (rb, 1), 0)) % T
                    col = lax.broadcasted_iota(jnp.int32, (1, T), 1)
                    s = jnp.where(row >= col, s, NEG)
                m_old = m_sc[h, rows, :]
                m_new = jnp.maximum(m_old, jnp.max(s, axis=1, keepdims=True))
                alpha = jnp.exp2(m_old - m_new)
                p = jnp.exp2(s - jnp.concatenate([m_new] * lane_tiles, axis=1))
                p_lanes = p[:, 0:ATT_W]
                for t in range(1, lane_tiles):
                    p_lanes = p_lanes + p[:, t * ATT_W:(t + 1) * ATT_W]
                l_sc[h, rows, :] = alpha * l_sc[h, rows, :] + p_lanes
                acc_sc[h, rows, :] = alpha * acc_sc[h, rows, :] + _dot(p.astype(BF16), v_h)
                m_sc[h, rows, :] = m_new

    @pl.when(ki < qi)
    def _():
        step(False)

    @pl.when(ki == qi)
    def _():
        step(True)
        for h in range(N_HEADS):
            on = acc_sc[h] / jnp.sum(l_sc[h], axis=1, keepdims=True)
            y = _diff_finish(on[:T], on[T:], lam_ref[...], gain_ref[:, heads[h]], out_scale)
            o_ref[:, heads[h]] = y.astype(o_ref.dtype)


def _attn_prompt(q, k, v, lam_row, gain, out_scale):
    B, L, _ = q.shape
    T = ATT_TILE if L % ATT_TILE == 0 else SEQ_TILE
    n = L // T
    pairs = [(a, b) for a in range(n) for b in range(a + 1)]
    qi_arr = jnp.asarray([a for a, _ in pairs], jnp.int32)
    ki_arr = jnp.asarray([b for _, b in pairs], jnp.int32)
    q_spec = pl.BlockSpec((None, T, ATT_ALL), lambda b, p, qi, ki: (b, qi[p], 0))
    kv_spec = pl.BlockSpec((None, T, ATT_ALL), lambda b, p, qi, ki: (b, ki[p], 0))
    grid_spec = pltpu.PrefetchScalarGridSpec(
        num_scalar_prefetch=2,
        grid=(B, len(pairs)),
        in_specs=[q_spec, kv_spec, kv_spec,
                  pl.BlockSpec((1, ATT_W), lambda b, p, qi, ki: (0, 0)),
                  pl.BlockSpec((1, ATT_ALL), lambda b, p, qi, ki: (0, 0))],
        out_specs=q_spec,
        scratch_shapes=[pltpu.VMEM((N_HEADS, 2 * T, ATT_W), BF16), pltpu.VMEM((N_HEADS, 2 * T, ATT_W), F32),
                        pltpu.VMEM((N_HEADS, 2 * T, ATT_W), F32), pltpu.VMEM((N_HEADS, 2 * T, ATT_W), F32)],
    )
    return pl.pallas_call(
        functools.partial(_attn_kernel, T, out_scale),
        grid_spec=grid_spec,
        out_shape=jax.ShapeDtypeStruct((B, L, ATT_ALL), BF16),
        compiler_params=_params("parallel", "arbitrary"),
        name="attn_prompt",
    )(qi_arr, ki_arr, q, k, v, lam_row, gain)


def _dec_attn_kernel(npg, ls, page, layer, out_scale, pt_ref, q_ref, kn_ref, vn_ref, lam_ref, gain_ref,
                     ck_hbm, cv_hbm, o_ref, kbuf, vbuf, ksem, vsem, qs_sc, m_sc, l_sc, acc_sc):
    b = pl.program_id(0)
    j = pl.program_id(1)
    n_steps = pl.num_programs(1)
    step = b * n_steps + j
    slot = step % 2
    rows = 2 * N_HEADS * ls
    cols = page * N_HEADS
    row_head = lax.broadcasted_iota(jnp.int32, (rows, 1), 0) // (2 * ls)

    def fetch(seq, blk, buf):
        for i in range(npg):
            pg = pt_ref[seq, blk * npg + i]
            pltpu.make_async_copy(ck_hbm.at[layer, pg], kbuf.at[buf, i], ksem.at[buf]).start()
            pltpu.make_async_copy(cv_hbm.at[layer, pg], vbuf.at[buf, i], vsem.at[buf]).start()

    @pl.when(step == 0)
    def _():
        fetch(0, 0, 0)

    @pl.when(step + 1 < pl.num_programs(0) * n_steps)
    def _():
        wrap = j + 1 == n_steps
        fetch(jnp.where(wrap, b + 1, b), jnp.where(wrap, 0, j + 1), 1 - slot)

    @pl.when(j == 0)
    def _():
        q = q_ref[...].astype(F32)
        lane = lax.broadcasted_iota(jnp.int32, (1, ATT_W), 1)
        parts = []
        for h in range(N_HEADS):
            qh = q[:, h * ATT_W:(h + 1) * ATT_W]
            parts += [jnp.where(lane < HEAD_W, qh, 0.0), jnp.where(lane >= HEAD_W, qh, 0.0)]
        qs_sc[...] = jnp.concatenate(parts, axis=0).astype(BF16)
        m_sc[...] = jnp.full(m_sc.shape, NEG, F32)
        l_sc[...] = jnp.zeros(l_sc.shape, F32)
        acc_sc[...] = jnp.zeros(acc_sc.shape, F32)

    qs = qs_sc[...]

    def update(scores, values):
        m_old = m_sc[...]
        m_new = m_old
        for s in scores:
            m_new = jnp.maximum(m_new, jnp.max(s, axis=1, keepdims=True))
        alpha = jnp.exp2(m_old - m_new)
        l_new = alpha * l_sc[...]
        acc = alpha * acc_sc[...]
        for s, val in zip(scores, values):
            p = jnp.exp2(s - m_new)
            l_new = l_new + jnp.sum(p, axis=1, keepdims=True)
            acc = acc + _dot(p.astype(BF16), val)
        l_sc[...] = l_new
        acc_sc[...] = acc
        m_sc[...] = m_new

    pltpu.make_async_copy(ck_hbm.at[layer, pl.ds(0, npg)], kbuf.at[slot], ksem.at[slot]).wait()
    pltpu.make_async_copy(cv_hbm.at[layer, pl.ds(0, npg)], vbuf.at[slot], vsem.at[slot]).wait()
    own_head = (lax.broadcasted_iota(jnp.int32, (1, cols), 1) % N_HEADS) == row_head
    update([jnp.where(own_head, _nt(qs, kbuf[slot, i].astype(BF16)), NEG) for i in range(npg)],
           [vbuf[slot, i].astype(BF16) for i in range(npg)])

    @pl.when(j == n_steps - 1)
    def _():
        n_new = ls * N_HEADS
        kn = _pad_rows(kn_ref[...], ATT_W).astype(BF16)
        vn = _pad_rows(vn_ref[...], ATT_W).astype(BF16)
        col = lax.broadcasted_iota(jnp.int32, (1, ATT_W), 1)
        q_idx = lax.broadcasted_iota(jnp.int32, (rows, 1), 0) % ls
        ok = (col % N_HEADS == row_head) & (col // N_HEADS <= q_idx) & (col < n_new)
        update([jnp.where(ok, _nt(qs, kn), NEG)], [vn])
        on = acc_sc[...] / l_sc[...]
        gain = gain_ref[...]
        parts = []
        for h in range(N_HEADS):
            r0 = h * 2 * ls
            parts.append(_diff_finish(on[r0:r0 + ls], on[r0 + ls:r0 + 2 * ls], lam_ref[...],
                                      gain[:, h * ATT_W:(h + 1) * ATT_W], out_scale))
        o_ref[...] = jnp.concatenate(parts, axis=1).astype(o_ref.dtype)


def _attn_sample(q, k_new, v_new, cache_k, cache_v, page_table, layer, lam_row, gain, out_scale):
    B, ls, _ = q.shape
    n_pages = page_table.shape[1]
    page = cache_k.shape[2]
    cols = page * N_HEADS
    npg = next(n for n in (PAGES_PER_STEP, 8, 4, 2, 1) if n_pages % n == 0)
    ck = cache_k.reshape(cache_k.shape[0], cache_k.shape[1], cols, ATT_W)
    cv = cache_v.reshape(cache_v.shape[0], cache_v.shape[1], cols, ATT_W)
    kn = k_new.reshape(B, ls * N_HEADS, ATT_W)
    vn = v_new.reshape(B, ls * N_HEADS, ATT_W)
    per_b = pl.BlockSpec((None, ls, ATT_ALL), lambda b, j, pt: (b, 0, 0))
    new_spec = pl.BlockSpec((None, ls * N_HEADS, ATT_W), lambda b, j, pt: (b, 0, 0))
    rows = 2 * N_HEADS * ls
    grid_spec = pltpu.PrefetchScalarGridSpec(
        num_scalar_prefetch=1,
        grid=(B, n_pages // npg),
        in_specs=[per_b, new_spec, new_spec,
                  pl.BlockSpec((1, ATT_W), lambda b, j, pt: (0, 0)),
                  pl.BlockSpec((1, ATT_ALL), lambda b, j, pt: (0, 0)),
                  pl.BlockSpec(memory_space=pl.ANY), pl.BlockSpec(memory_space=pl.ANY)],
        out_specs=per_b,
        scratch_shapes=[pltpu.VMEM((2, npg, cols, ATT_W), F32), pltpu.VMEM((2, npg, cols, ATT_W), F32),
                        pltpu.SemaphoreType.DMA((2,)), pltpu.SemaphoreType.DMA((2,)),
                        pltpu.VMEM((rows, ATT_W), BF16), pltpu.VMEM((rows, 1), F32), pltpu.VMEM((rows, 1), F32),
                        pltpu.VMEM((rows, ATT_W), F32)],
    )
    return pl.pallas_call(
        functools.partial(_dec_attn_kernel, npg, ls, page, layer, out_scale),
        grid_spec=grid_spec,
        out_shape=jax.ShapeDtypeStruct((B, ls, ATT_ALL), BF16),
        compiler_params=_params("arbitrary", "arbitrary"),
        name="attn_sample",
    )(page_table, q, kn, vn, lam_row, gain, ck, cv)


def _outproj_kernel(alpha, ma_ref, mb_ref, mc_ref, h_ref, wo_ref, g_ref, b_ref, wr_ref, br_ref,
                    h1_ref, h1t_ref, eid_ref, gate_ref):
    W = REC_W
    mix = (_dot(ma_ref[...], wo_ref[0:W, :]) + _dot(mb_ref[...], wo_ref[W:W + ATT_ALL, :])
           + _dot(mc_ref[...], wo_ref[W + ATT_ALL:2 * W + ATT_ALL, :]))
    h1 = _layernorm(alpha * h_ref[...] + mix, g_ref[...], b_ref[...])
    h1_ref[...] = h1
    _to_row_tiles(h1t_ref, h1)

    logits = _dot(h1.astype(BF16), wr_ref[...]) + br_ref[...]
    lane = lax.broadcasted_iota(jnp.int32, (1, ROUTE_LANES), 1)
    lane_f = lane.astype(F32)
    far = float(ROUTE_LANES)

    gl = jnp.where(lane < N_GROUPS, logits, NEG)
    g_max = jnp.max(gl, axis=1, keepdims=True)
    g_top = 1.0 / jnp.sum(jnp.exp(gl - g_max), axis=1, keepdims=True)
    g_idx = jnp.min(jnp.where(gl == g_max, lane_f, far), axis=1, keepdims=True)

    e_lane = lane_f - float(EXPERT_LANE0)
    in_group = (e_lane >= g_idx * GROUP_EXPERTS) & (e_lane < (g_idx + 1.0) * GROUP_EXPERTS)
    el = jnp.where(in_group, logits, NEG)
    e_max = jnp.max(el, axis=1, keepdims=True)
    e_sum = jnp.sum(jnp.exp(el - e_max), axis=1, keepdims=True)
    idx1 = jnp.min(jnp.where(el == e_max, lane_f, far), axis=1, keepdims=True)
    el2 = jnp.where(lane_f == idx1, NEG, el)
    e_max2 = jnp.max(el2, axis=1, keepdims=True)
    idx2 = jnp.min(jnp.where(el2 == e_max2, lane_f, far), axis=1, keepdims=True)
    p1 = 1.0 / e_sum
    p2 = jnp.exp(e_max2 - e_max) / e_sum
    tot = p1 + p2
    gate_ref[...] = jnp.where(lane == 0, g_top * (p1 / tot), jnp.where(lane == 1, g_top * (p2 / tot), 0.0))
    eid = jnp.where(lane == 0, idx1 - float(EXPERT_LANE0), jnp.where(lane == 1, idx2 - float(EXPERT_LANE0), 0.0))
    eid_ref[...] = eid.astype(jnp.int32)


def _outproj(ma, mb, mc, h, wo, g, b, wr, br, alpha):
    T, D = h.shape
    tm = next((t for t in (ROW_TILE, 256, 128) if T % t == 0), T)
    tile = lambda n: pl.BlockSpec((tm, n), lambda i: (i, 0))
    const = lambda r, n: pl.BlockSpec((r, n), lambda i: (0, 0))
    return pl.pallas_call(
        functools.partial(_outproj_kernel, alpha),
        grid=(T // tm,),
        in_specs=[tile(REC_W), tile(ATT_ALL), tile(REC_W), tile(D), const(2 * REC_W + ATT_ALL, D),
                  const(1, D), const(1, D), const(D, ROUTE_LANES), const(1, ROUTE_LANES)],
        out_specs=[tile(D), pl.BlockSpec((tm, D // LANES, LANES), lambda i: (i, 0, 0)), tile(ROUTE_LANES),
                   tile(ROUTE_LANES)],
        out_shape=[jax.ShapeDtypeStruct((T, D), F32), jax.ShapeDtypeStruct((T, D // LANES, LANES), F32),
                   jax.ShapeDtypeStruct((T, ROUTE_LANES), jnp.int32),
                   jax.ShapeDtypeStruct((T, ROUTE_LANES), F32)],
        compiler_params=_params("parallel"),
        name="outproj",
    )(ma, mb, mc, h, wo, g, b, wr, br)


def _expert_kernel(row_ref, blke_ref, nval_ref, x_hbm, wg_ref, wu_ref, wd_ref, out_hbm,
                   xbuf, ybuf, gsem, ssem):
    del blke_ref
    i = pl.program_id(0)
    nblk = pl.num_programs(0)
    cur = i % 2

    def start_gather(blk, buf):
        @pl.when(nval_ref[blk] > 0)
        def _():
            def body(r, carry):
                tok = lax.shift_right_logical(row_ref[blk * MOE_ROWS + r], 1)
                pltpu.make_async_copy(x_hbm.at[pl.ds(tok, 1)], xbuf.at[buf, pl.ds(r, 1)], gsem.at[buf]).start()
                return carry
            lax.fori_loop(0, MOE_ROWS, body, 0, unroll=SUB)

    def start_scatter(blk, buf):
        def one(r):
            code = row_ref[blk * MOE_ROWS + r]
            pltpu.make_async_copy(ybuf.at[buf, pl.ds(r, 1)],
                                  out_hbm.at[code & 1, pl.ds(lax.shift_right_logical(code, 1), 1)],
                                  ssem.at[buf]).start()

        n = nval_ref[blk]
        full = lax.shift_right_logical(n, 3)

        def group(g, carry):
            for u in range(SUB):
                one(g * SUB + u)
            return carry
        lax.fori_loop(0, full, group, 0)

        def rest(r, carry):
            one(r)
            return carry
        lax.fori_loop(full * SUB, n, rest, 0)

    def wait_scatter(blk, buf):
        n = nval_ref[blk]

        @pl.when(n > 0)
        def _():
            pltpu.make_async_copy(ybuf.at[buf, pl.ds(0, n)], out_hbm.at[0, pl.ds(0, n)], ssem.at[buf]).wait()

    @pl.when(i == 0)
    def _():
        start_gather(0, 0)

    @pl.when(i + 1 < nblk)
    def _():
        start_gather(i + 1, 1 - cur)

    @pl.when(i >= 2)
    def _():
        wait_scatter(i - 2, cur)

    @pl.when(nval_ref[i] > 0)
    def _():
        pltpu.make_async_copy(x_hbm.at[pl.ds(0, MOE_ROWS)], xbuf.at[cur], gsem.at[cur]).wait()
        xb = _from_row_tiles(xbuf.at[cur]).astype(BF16)
        hid = _silu(_dot(xb, wg_ref[...])) * _dot(xb, wu_ref[...])
        _to_row_tiles(ybuf.at[cur], _dot(hid.astype(BF16), wd_ref[...]))
        start_scatter(i, cur)

    @pl.when(i == nblk - 1)
    def _():
        @pl.when(i >= 1)
        def _():
            wait_scatter(i - 1, 1 - cur)
        wait_scatter(i, cur)


def _experts(x_tiles, row_code, blk_e, blk_nval, wg, wu, wd):
    T, chunks, _ = x_tiles.shape
    D = chunks * LANES
    n_blocks = blk_e.shape[0]
    de = wg.shape[2]
    grid_spec = pltpu.PrefetchScalarGridSpec(
        num_scalar_prefetch=3,
        grid=(n_blocks,),
        in_specs=[pl.BlockSpec(memory_space=pl.ANY),
                  pl.BlockSpec((None, D, de), lambda i, code, be, nv: (be[i], 0, 0)),
                  pl.BlockSpec((None, D, de), lambda i, code, be, nv: (be[i], 0, 0)),
                  pl.BlockSpec((None, de, D), lambda i, code, be, nv: (be[i], 0, 0))],
        out_specs=pl.BlockSpec(memory_space=pl.ANY),
        scratch_shapes=[pltpu.VMEM((2, MOE_ROWS, chunks, LANES), F32), pltpu.VMEM((2, MOE_ROWS, chunks, LANES), F32),
                        pltpu.SemaphoreType.DMA((2,)), pltpu.SemaphoreType.DMA((2,))],
    )
    return pl.pallas_call(
        _expert_kernel,
        grid_spec=grid_spec,
        out_shape=jax.ShapeDtypeStruct((2, T, chunks, LANES), F32),
        compiler_params=_params("arbitrary"),
        name="experts",
    )(row_code, blk_e, blk_nval, x_tiles, wg, wu, wd)


def _dispatch(eid, T):
    K = 2
    A = T * K
    flat_e = eid[:, :K].reshape(-1)
    onehot = (flat_e[:, None] == jnp.arange(N_EXPERTS, dtype=jnp.int32)[None, :]).astype(F32)
    a_pad = -(-A // MOE_ROWS) * MOE_ROWS
    oh3 = jnp.pad(onehot, ((0, a_pad - A), (0, 0))).reshape(a_pad // MOE_ROWS, MOE_ROWS, N_EXPERTS)
    tril = jnp.tril(jnp.ones((MOE_ROWS, MOE_ROWS), F32))
    within = jnp.einsum('ts,bse->bte', tril, oh3, precision=HIGHEST)
    blk_tot = within[:, -1, :]
    before = jnp.cumsum(blk_tot, axis=0) - blk_tot
    rank = jnp.sum((within + before[:, None, :]) * oh3, axis=-1).reshape(a_pad)[:A] - 1.0
    counts = jnp.sum(blk_tot, axis=0).astype(jnp.int32)
    padded = (counts + MOE_ROWS - 1) // MOE_ROWS * MOE_ROWS
    pad_end = jnp.cumsum(padded)
    pad_start = pad_end - padded
    dest = jnp.sum(onehot * pad_start.astype(F32)[None, :], axis=-1) + rank
    n_blocks = -(-A // MOE_ROWS) + N_EXPERTS
    R = n_blocks * MOE_ROWS
    row_code = jnp.zeros((R,), jnp.int32).at[dest.astype(jnp.int32)].set(jnp.arange(A, dtype=jnp.int32))
    blk_start = jnp.arange(n_blocks, dtype=jnp.int32) * MOE_ROWS
    blk_e = jnp.minimum(jnp.sum((pad_end[None, :] <= blk_start[:, None]).astype(jnp.int32), axis=1), N_EXPERTS - 1)
    blk_nval = jnp.clip((pad_start + counts)[blk_e] - blk_start, 0, MOE_ROWS).astype(jnp.int32)
    return row_code, blk_e.astype(jnp.int32), blk_nval


def _combine_kernel(alpha, h_ref, y_ref, gate_ref, g_ref, b_ref, o_ref):
    gate = gate_ref[...]
    y = gate[:, 0:1] * _from_row_tiles(y_ref.at[0]) + gate[:, 1:2] * _from_row_tiles(y_ref.at[1])
    o_ref[...] = _layernorm(alpha * h_ref[...] + y, g_ref[...], b_ref[...])


def _combine(h1, slots, gates, g, b, alpha):
    T, D = h1.shape
    tm = next((t for t in (ROW_TILE, 256, 128) if T % t == 0), T)
    return pl.pallas_call(
        functools.partial(_combine_kernel, alpha),
        grid=(T // tm,),
        in_specs=[pl.BlockSpec((tm, D), lambda i: (i, 0)), pl.BlockSpec((2, tm, D // LANES, LANES), lambda i: (0, i, 0, 0)),
                  pl.BlockSpec((tm, ROUTE_LANES), lambda i: (i, 0)),
                  pl.BlockSpec((1, D), lambda i: (0, 0)), pl.BlockSpec((1, D), lambda i: (0, 0))],
        out_specs=pl.BlockSpec((tm, D), lambda i: (i, 0)),
        out_shape=jax.ShapeDtypeStruct((T, D), F32),
        compiler_params=_params("parallel"),
        name="combine",
    )(h1, slots, gates, g, b)


def _block_diag(blocks):
    B, H, n, _ = blocks.shape
    eye = jnp.eye(H, dtype=blocks.dtype)
    return jnp.einsum('bhij,hg->bhigj', blocks, eye).reshape(B, H * n, H * n)


def _diag_blocks(mat, H):
    B, n = mat.shape[0], mat.shape[1] // H
    m5 = mat.reshape(B, H, n, H, n)
    return jnp.stack([m5[:, h, :, h, :] for h in range(H)], axis=1)


def _layer(x, do_ln, B, L, l_real, lw, state, past):
    n_seq, l_out = (B, l_real) if past is None else (1, B * L)
    outs = _inproj(x, lw['ln_g'], lw['ln_b'], lw['w_in'], lw['b_in'], do_ln, n_seq, l_out)
    a, qb, kb, vb, cg, gates, kb16, vb16 = outs[:8]
    h = outs[8] if do_ln else x
    s3 = lambda z: z.reshape(B, L, z.shape[-1])

    mix_a, st_a = _hgrn(s3(a), lw['lb'], lw['norm_a'], state['hgrn'], l_real)
    mix_c, ct, n_c, m_c = _mlstm(s3(cg), s3(gates), lw['conv_w'], lw['conv_b'], lw['b_f'], lw['norm_c'],
                                 state['c'], state['n'], state['m'], state['tail'], l_real)
    if past is None:
        mix_b = _attn_prompt(s3(qb), s3(kb16), s3(vb16), lw['lam'], lw['norm_b'], lw['out_scale'])
    else:
        mix_b = _attn_sample(s3(qb), kb, vb, past[0], past[1], past[2], lw['layer'], lw['lam'],
                             lw['norm_b'], lw['out_scale'])

    flat = lambda z: z.reshape(B * L, z.shape[-1])
    h1, h1_tiles, eid, gate = _outproj(flat(mix_a), flat(mix_b), flat(mix_c), h, lw['w_out'], lw['ln1_g'], lw['ln1_b'],
                             lw['w_route'], lw['b_route'], lw['alpha'])
    row_code, blk_e, blk_nval = _dispatch(eid, B * L)
    slots = _experts(h1_tiles, row_code, blk_e, blk_nval, lw['w_eg'], lw['w_eu'], lw['w_ed'])
    h2 = _combine(h1, slots, gate, lw['ln2_g'], lw['ln2_b'], lw['alpha'])

    new_state = {
        'k': kb.reshape(B, l_real, N_HEADS, ATT_W), 'v': vb.reshape(B, l_real, N_HEADS, ATT_W),
        'hgrn': jnp.swapaxes(_diag_blocks(st_a, N_HEADS), -1, -2),
        'c': _diag_blocks(ct, N_HEADS),
        'n': n_c.reshape(B, N_HEADS, HEAD_W),
        'm': m_c.reshape(B, N_HEADS, HEAD_W)[:, :, 0],
        'conv': s3(cg)[:, l_real - (CONV_TAPS - 1):l_real, :2 * REC_W],
    }
    return h2, new_state


def kernel(x_prompt, x_sample, cache_k, cache_v, page_table, state_hgrn, state_mlstm_c, state_mlstm_n, state_mlstm_m, state_mlstm_conv, meta_tokens, ln_emb_g, ln_emb_b, w_in, b_in, b_mlstm_f, conv_w, conv_b, hgrn_lb_logits, lambda_q1, lambda_k1, lambda_q2, lambda_k2, norm_a, norm_b, norm_c, w_out, ln1_g, ln1_b, w_router_group, b_router_group, w_router_expert, b_router_expert, w_exp_gate, w_exp_up, w_exp_down, ln2_g, ln2_b):
    depth = w_in.shape[0]
    Bp, Lq, D = x_prompt.shape
    Bs, Ls, _ = x_sample.shape
    n_meta = meta_tokens.shape[0]
    l_real = n_meta + Lq
    Lp = -(-l_real // SEQ_TILE) * SEQ_TILE
    alpha = (2.0 * depth) ** 0.25
    row = lambda z: z.reshape(1, -1).astype(F32)

    meta = jnp.broadcast_to(meta_tokens[None].astype(x_prompt.dtype), (Bp, n_meta, D))
    xp = jnp.concatenate([meta, x_prompt, jnp.zeros((Bp, Lp - l_real, D), x_prompt.dtype)], axis=1)
    hp = xp.reshape(Bp * Lp, D)
    hs = x_sample.reshape(Bs * Ls, D)

    lb_sm = jax.nn.softmax(hgrn_lb_logits.astype(F32), axis=0)
    lb_all = jnp.cumsum(lb_sm, axis=0) - lb_sm[0]

    zeros_p = {
        'hgrn': jnp.zeros((Bp, REC_W, REC_W), F32), 'c': jnp.zeros((Bp, REC_W, REC_W), F32),
        'n': jnp.zeros((Bp, 1, REC_W), F32), 'm': jnp.zeros((Bp, 1, REC_W), F32),
        'tail': jnp.zeros((Bp, SUB, 2 * REC_W), F32),
    }

    res_p, res_s = [], []
    for l in range(depth):
        lam_init = 0.8 - 0.6 * math.exp(-0.3 * l)
        lam = (jnp.exp(jnp.sum(lambda_q1[l].astype(F32) * lambda_k1[l].astype(F32)))
               - jnp.exp(jnp.sum(lambda_q2[l].astype(F32) * lambda_k2[l].astype(F32))) + lam_init)
        pad_cols = N_PROJ_PAD - N_PROJ_RAW
        bias_f = jnp.zeros((ROUTE_LANES,), F32).at[N_HEADS:2 * N_HEADS].set(b_mlstm_f[l].astype(F32))
        w_route = jnp.concatenate([w_router_group[l], w_router_expert[l]], axis=1)
        b_route = jnp.concatenate([b_router_group[l], b_router_expert[l]])
        r_pad = ROUTE_LANES - w_route.shape[1]
        lw = {
            'layer': l, 'alpha': alpha, 'out_scale': 1.0 - lam_init,
            'ln_g': row(ln_emb_g), 'ln_b': row(ln_emb_b),
            'w_in': jnp.pad(w_in[l], ((0, 0), (0, pad_cols))).astype(BF16),
            'b_in': row(jnp.pad(b_in[l], (0, pad_cols))),
            'lb': row(lb_all[l]), 'norm_a': row(norm_a[l]), 'norm_b': row(norm_b[l]), 'norm_c': row(norm_c[l]),
            'lam': jnp.full((1, ATT_W), lam, F32),
            'conv_w': conv_w[l].astype(F32), 'conv_b': row(conv_b[l]), 'b_f': row(bias_f),
            'w_out': w_out[l].astype(BF16), 'ln1_g': row(ln1_g[l]), 'ln1_b': row(ln1_b[l]),
            'w_route': jnp.pad(w_route, ((0, 0), (0, r_pad))).astype(BF16), 'b_route': row(jnp.pad(b_route, (0, r_pad))),
            'w_eg': w_exp_gate[l].astype(BF16), 'w_eu': w_exp_up[l].astype(BF16), 'w_ed': w_exp_down[l].astype(BF16),
            'ln2_g': row(ln2_g[l]), 'ln2_b': row(ln2_b[l]),
        }
        hp, st_p = _layer(hp, l == 0, Bp, Lp, l_real, lw, zeros_p, None)

        state_s = {
            'hgrn': _block_diag(jnp.swapaxes(state_hgrn[l].astype(F32), -1, -2)),
            'c': _block_diag(state_mlstm_c[l].astype(F32)),
            'n': state_mlstm_n[l].astype(F32).reshape(Bs, 1, REC_W),
            'm': jnp.repeat(state_mlstm_m[l].astype(F32), HEAD_W, axis=-1).reshape(Bs, 1, REC_W),
            'tail': jnp.pad(state_mlstm_conv[l].astype(F32), ((0, 0), (SUB - (CONV_TAPS - 1), 0), (0, 0))),
        }
        hs, st_s = _layer(hs, l == 0, Bs, Ls, Ls, lw, state_s, (cache_k, cache_v, page_table))
        res_p.append(st_p)
        res_s.append(st_s)

    y_prompt = hp.reshape(Bp, Lp, D)[:, n_meta:l_real]
    y_sample = hs.reshape(Bs, Ls, D)
    stack = lambda res, key, f=lambda z: z: jnp.stack([f(r[key]) for r in res])
    return (y_prompt, y_sample,
            stack(res_p, 'k'), stack(res_p, 'v'), stack(res_s, 'k'), stack(res_s, 'v'),
            stack(res_p, 'hgrn'), stack(res_s, 'hgrn'),
            stack(res_p, 'c'), stack(res_s, 'c'),
            stack(res_p, 'n'), stack(res_s, 'n'),
            stack(res_p, 'm'), stack(res_s, 'm'),
            stack(res_p, 'conv'), stack(res_s, 'conv'))
```
